```python
import math
import jax
import jax.numpy as jnp
from jax import lax
import numpy as np

D_MODEL = 1024
BATCH = 16
SEQ = 2048
DEPTH = 2
DEC_BATCH = 128
DEC_SEQ = 1
PAST_LEN = 8192
PAGE_SIZE = 128

N_META = 16
HEAD_DIM = 64
MIX_WIDTH = D_MODEL
GROUP_WIDTH = MIX_WIDTH // 4
Q_BLOCK = 128
NORM_EPS = 1e-6

FOX_HEADS = GROUP_WIDTH // HEAD_DIM
FOX_DH = HEAD_DIM
FOX_SCALE = FOX_DH ** -0.5

S5_WIDTH = GROUP_WIDTH
S5_GC = 16
S5_GROUPS = S5_WIDTH // S5_GC
S5_STATE = 64
S5_DT_MIN = 1e-3
S5_DT_MAX = 1e-1

GLA_HEADS = GROUP_WIDTH // HEAD_DIM
GLA_DK = HEAD_DIM
GLA_DV = HEAD_DIM
GLA_LOWRANK = 16
GLA_GATE_NORM = 16.0
GLA_CHUNK = 64

MLA_HEADS = GROUP_WIDTH // HEAD_DIM
MLA_NOPE = HEAD_DIM
MLA_ROPE = HEAD_DIM // 2
MLA_V = HEAD_DIM
MLA_Q_RANK = 3 * GROUP_WIDTH // 4
MLA_KV_RANK = GROUP_WIDTH // 2
MLA_SCALE = (MLA_NOPE + MLA_ROPE) ** -0.5
ROPE_THETA = 10000.0

D_FF = -(-8 * D_MODEL // (3 * 256)) * 256

IN_SIZES = (
    FOX_HEADS * FOX_DH, FOX_HEADS * FOX_DH, FOX_HEADS * FOX_DH, FOX_HEADS,
    S5_WIDTH,
    GLA_HEADS * GLA_DK, GLA_HEADS * GLA_DK, GLA_HEADS * GLA_DV, GLA_HEADS * GLA_DV,
    GLA_LOWRANK,
    MLA_Q_RANK, MLA_KV_RANK, MLA_ROPE,
)
IN_COLS = sum(IN_SIZES)

kernel_name = 'hybrid_fox_s5_gla_mla_decoder_step'

F32 = jnp.float32


def rms_norm(x, w):
    xf = x.astype(F32)
    y = xf * lax.rsqrt(jnp.mean(xf * xf, axis=-1, keepdims=True) + NORM_EPS)
    return (y * w.astype(F32)).astype(x.dtype)


def split_last(x, sizes):
    idx = np.cumsum(sizes)[:-1].tolist()
    return jnp.split(x, idx, axis=-1)


def rope(x, pos):
    r = x.shape[-1]
    half = r // 2
    inv = ROPE_THETA ** (-jnp.arange(half, dtype=F32) * 2.0 / r)
    ang = pos.astype(F32)[:, None] * inv[None, :]
    shape = (pos.shape[0],) + (1,) * (x.ndim - 3) + (half,)
    cos = jnp.cos(ang).reshape(shape)
    sin = jnp.sin(ang).reshape(shape)
    xf = x.astype(F32)
    x1, x2 = xf[..., :half], xf[..., half:]
    return jnp.concatenate([x1 * cos - x2 * sin, x2 * cos + x1 * sin], axis=-1).astype(x.dtype)


def decay_bias(cq, ck):
    return jnp.swapaxes(cq, 1, 2)[..., :, None] - jnp.swapaxes(ck, 1, 2)[..., None, :]


def joint_softmax(s_past, s_new):
    n_past = s_past.shape[-1]
    p = jax.nn.softmax(jnp.concatenate([s_past, s_new], axis=-1), axis=-1)
    return p[..., :n_past], p[..., n_past:]


def gather_pages(pool, page_table):
    g = pool[page_table]
    return g.reshape((page_table.shape[0], page_table.shape[1] * pool.shape[1]) + pool.shape[2:])


def prompt_attention(q, k, v, scale, c=None):
    b_, l_, h_, dq = q.shape
    pos = jnp.arange(l_)

    def block(q_blk, q_pos, c_blk):
        s = jnp.einsum('bqhd,bkhd->bhqk', q_blk, k).astype(F32) * scale
        if c is not None:
            s = s + decay_bias(c_blk, c)
        s = jnp.where(q_pos[:, None] >= pos[None, :], s, -jnp.inf)
        p = jax.nn.softmax(s, axis=-1).astype(v.dtype)
        return jnp.einsum('bhqk,bkhd->bqhd', p, v)

    o_meta = block(q[:, :N_META], pos[:N_META], None if c is None else c[:, :N_META])
    n_blk = (l_ - N_META) // Q_BLOCK
    qb = q[:, N_META:].reshape(b_, n_blk, Q_BLOCK, h_, dq).swapaxes(0, 1)
    pb = pos[N_META:].reshape(n_blk, Q_BLOCK)
    if c is None:
        ob = lax.map(lambda a: block(a[0], a[1], None), (qb, pb))
    else:
        cb = c[:, N_META:].reshape(b_, n_blk, Q_BLOCK, h_).swapaxes(0, 1)
        ob = lax.map(lambda a: block(a[0], a[1], a[2]), (qb, pb, cb))
    o_real = ob.swapaxes(0, 1).reshape(b_, n_blk * Q_BLOCK, h_, v.shape[-1])
    return jnp.concatenate([o_meta, o_real], axis=1)


def fox_sample_attention(q, k, v, logf, k_past, v_past, logf_past):
    t_ = q.shape[1]
    lfp = logf_past.astype(F32)
    suffix = jnp.flip(jnp.cumsum(jnp.flip(lfp, 1), axis=1), 1) - lfp
    c_new = jnp.cumsum(logf, axis=1)
    s_past = jnp.einsum('bqhd,bkhd->bhqk', q, k_past).astype(F32) * FOX_SCALE + decay_bias(c_new, -suffix)
    s_new = jnp.einsum('bqhd,bkhd->bhqk', q, k).astype(F32) * FOX_SCALE + decay_bias(c_new, c_new)
    s_new = jnp.where(jnp.tril(jnp.ones((t_, t_), bool)), s_new, -jnp.inf)
    p_past, p_new = joint_softmax(s_past, s_new)
    return (jnp.einsum('bhqk,bkhd->bqhd', p_past.astype(v.dtype), v_past)
            + jnp.einsum('bhqk,bkhd->bqhd', p_new.astype(v.dtype), v))


def mla_sample_attention(q_nope, q_rope, ckv, krope, ckv_past, krope_past, w_uk, w_uv):
    t_ = q_nope.shape[1]
    q_abs = jnp.einsum('bthd,chd->bthc', q_nope, w_uk)
    s_past = (jnp.einsum('bthc,bsc->bhts', q_abs, ckv_past)
              + jnp.einsum('bthr,bsr->bhts', q_rope, krope_past)).astype(F32) * MLA_SCALE
    s_new = (jnp.einsum('bthc,bsc->bhts', q_abs, ckv)
             + jnp.einsum('bthr,bsr->bhts', q_rope, krope)).astype(F32) * MLA_SCALE
    s_new = jnp.where(jnp.tril(jnp.ones((t_, t_), bool)), s_new, -jnp.inf)
    p_past, p_new = joint_softmax(s_past, s_new)
    o_lat = (jnp.einsum('bhts,bsc->bthc', p_past.astype(ckv.dtype), ckv_past)
             + jnp.einsum('bhts,bsc->bthc', p_new.astype(ckv.dtype), ckv))
    return jnp.einsum('bthc,chd->bthd', o_lat, w_uv)


def s5_discretize(a_re, a_im, log_dt, b_re, b_im):
    dt = jnp.exp(log_dt.astype(F32))[:, None]
    ar, ai = a_re.astype(F32), a_im.astype(F32)
    mag = jnp.exp(ar * dt)
    abar_re, abar_im = mag * jnp.cos(ai * dt), mag * jnp.sin(ai * dt)
    nr, ni = abar_re - 1.0, abar_im
    den = ar * ar + ai * ai
    fr, fi = (nr * ar + ni * ai) / den, (ni * ar - nr * ai) / den
    br, bi = b_re.astype(F32), b_im.astype(F32)
    bbar_re = fr[..., None] * br - fi[..., None] * bi
    bbar_im = fr[..., None] * bi + fi[..., None] * br
    return abar_re, abar_im, bbar_re, bbar_im


def s5_mixer(u, h0_re, h0_im, lp):
    b_, t_, _ = u.shape
    uf = u.astype(F32)
    ug = uf.reshape(b_, t_, S5_GROUPS, S5_GC)
    ar, ai, br, bi = s5_discretize(lp['s5_a_re'], lp['s5_a_im'], lp['s5_log_dt'], lp['s5_b_re'], lp['s5_b_im'])
    bu_re = jnp.einsum('gpc,btgc->btgp', br, ug)
    bu_im = jnp.einsum('gpc,btgc->btgp', bi, ug)
    a_re = jnp.broadcast_to(ar, bu_re.shape)
    a_im = jnp.broadcast_to(ai, bu_im.shape)

    def combine(e1, e2):
        a1r, a1i, b1r, b1i = e1
        a2r, a2i, b2r, b2i = e2
        return (a2r * a1r - a2i * a1i, a2r * a1i + a2i * a1r,
                a2r * b1r - a2i * b1i + b2r, a2r * b1i + a2i * b1r + b2i)

    acr, aci, bcr, bci = lax.associative_scan(combine, (a_re, a_im, bu_re, bu_im), axis=1)
    h0r = h0_re.astype(F32)[:, None]
    h0i = h0_im.astype(F32)[:, None]
    h_re = acr * h0r - aci * h0i + bcr
    h_im = acr * h0i + aci * h0r + bci
    y = (jnp.einsum('gcp,btgp->btgc', lp['s5_c_re'].astype(F32), h_re)
         - jnp.einsum('gcp,btgp->btgc', lp['s5_c_im'].astype(F32), h_im))
    y = y.reshape(b_, t_, S5_WIDTH) + lp['s5_d'].astype(F32) * uf
    g = jax.nn.gelu(y)
    out = g * jax.nn.sigmoid(g @ lp['w_s5_glu'].astype(F32) + lp['b_s5_glu'].astype(F32))
    return out.astype(u.dtype), h_re[:, -1], h_im[:, -1]


def gla_chunked(q, k, v, gk, s0):
    b_, t_, h_, dk = q.shape
    c_ = min(GLA_CHUNK, t_)
    n = -(-t_ // c_)
    pad = n * c_ - t_

    def to_chunks(a):
        a = jnp.pad(a, ((0, 0), (0, pad), (0, 0), (0, 0)))
        return a.reshape(b_, n, c_, h_, a.shape[-1]).swapaxes(0, 1)

    mask = jnp.tril(jnp.ones((c_, c_), bool))[None, :, :, None, None]

    def step(s, xs):
        qc, kc, vc, gc = xs
        bcum = jnp.cumsum(gc, axis=1)
        o_inter = jnp.einsum('bthk,bhkv->bthv', qc * jnp.exp(bcum), s)
        diff = bcum[:, :, None] - bcum[:, None, :]
        decay = jnp.exp(jnp.where(mask, diff, -jnp.inf))
        att = jnp.einsum('bthk,bshk,btshk->bhts', qc, kc, decay)
        o_intra = jnp.einsum('bhts,bshv->bthv', att, vc)
        b_last = bcum[:, -1]
        s_new = (jnp.exp(b_last)[..., None] * s
                 + jnp.einsum('bshk,bshv->bhkv', kc * jnp.exp(b_last[:, None] - bcum), vc))
        return s_new, o_inter + o_intra

    s_fin, o = lax.scan(step, s0, (to_chunks(q), to_chunks(k), to_chunks(v), to_chunks(gk)))
    o = o.swapaxes(0, 1).reshape(b_, n * c_, h_, v.shape[-1])[:, :t_]
    return o, s_fin


def project(hn, lp, pos):
    b_, t_, _ = hn.shape
    z = hn @ lp['w_in']
    fq, fk, fv, ff, su, gq, gk, gv, gg, glr, mq, mkv, mkr = split_last(z, IN_SIZES)

    def heads(a, h):
        return a.reshape(b_, t_, h, -1)

    fox_logf = jax.nn.log_sigmoid((ff + lp['b_fox_f']).astype(F32))
    fox = (heads(fq, FOX_HEADS), heads(fk, FOX_HEADS), heads(fv, FOX_HEADS), fox_logf)
    gla_log = jax.nn.log_sigmoid((glr @ lp['w_gla_gk'] + lp['b_gla_gk']).astype(F32)) / GLA_GATE_NORM
    gla = (heads(gq, GLA_HEADS).astype(F32) * GLA_DK ** -0.5, heads(gk, GLA_HEADS).astype(F32),
           heads(gv, GLA_HEADS).astype(F32), heads(gla_log, GLA_HEADS), gg)
    qf = heads(rms_norm(mq, lp['mla_q_norm']) @ lp['w_mla_uq'], MLA_HEADS)
    mla = (qf[..., :MLA_NOPE], rope(qf[..., MLA_NOPE:], pos), rms_norm(mkv, lp['mla_kv_norm']), rope(mkr, pos))
    return fox, su, gla, mla


def mix_output(fox_o, s5_o, gla_o, gla_g, mla_o, lp, dtype):
    b_, t_ = s5_o.shape[:2]
    gla_o = rms_norm(gla_o, lp['gla_norm']) * jax.nn.silu(gla_g.astype(F32)).reshape(b_, t_, GLA_HEADS, GLA_DV)
    cat = jnp.concatenate([fox_o.reshape(b_, t_, -1).astype(dtype), s5_o.astype(dtype),
                           gla_o.reshape(b_, t_, -1).astype(dtype), mla_o.reshape(b_, t_, -1).astype(dtype)], axis=-1)
    return cat @ lp['w_out']


def finish(x, mix, lp):
    x = x + rms_norm(mix, lp['norm_mix_post'])
    hf = rms_norm(x, lp['norm_ffn_pre'])
    f = (jax.nn.silu(hf @ lp['w_ffn_gate']) * (hf @ lp['w_ffn_up'])) @ lp['w_ffn_down']
    return x + rms_norm(f, lp['norm_ffn_post'])


def layer_prompt(x, lp):
    b_, t_, _ = x.shape
    pos = jnp.arange(t_)
    hn = rms_norm(x, lp['norm_mix_pre'])
    (fq, fk, fv, flogf), su, (gq, gk, gv, glog, gout), (qn, qr, ckv, kr) = project(hn, lp, pos)
    fox_o = prompt_attention(fq, fk, fv, FOX_SCALE, jnp.cumsum(flogf, axis=1))
    h0 = jnp.zeros((b_, S5_GROUPS, S5_STATE), F32)
    s5_o, s5_re, s5_im = s5_mixer(su, h0, h0, lp)
    s0 = jnp.zeros((b_, GLA_HEADS, GLA_DK, GLA_DV), F32)
    o_meta, s_meta = gla_chunked(gq[:, :N_META], gk[:, :N_META], gv[:, :N_META], glog[:, :N_META], s0)
    o_real, s_fin = gla_chunked(gq[:, N_META:], gk[:, N_META:], gv[:, N_META:], glog[:, N_META:], s_meta)
    gla_o = jnp.concatenate([o_meta, o_real], axis=1)
    k_mla = jnp.concatenate([jnp.einsum('btc,chd->bthd', ckv, lp['w_mla_uk']),
                             jnp.broadcast_to(kr[:, :, None, :], (b_, t_, MLA_HEADS, MLA_ROPE))], axis=-1)
    v_mla = jnp.einsum('btc,chd->bthd', ckv, lp['w_mla_uv'])
    mla_o = prompt_attention(jnp.concatenate([qn, qr], axis=-1), k_mla, v_mla, MLA_SCALE)
    mix = mix_output(fox_o, s5_o, gla_o, gout, mla_o, lp, x.dtype)
    return finish(x, mix, lp), (fk, fv, flogf, ckv, kr, s5_re, s5_im, s_fin)


def layer_sample(x, lp, k_past, v_past, logf_past, ckv_past, kr_past, h_re, h_im, s_gla):
    t_ = x.shape[1]
    pos = k_past.shape[1] + jnp.arange(t_)
    hn = rms_norm(x, lp['norm_mix_pre'])
    (fq, fk, fv, flogf), su, (gq, gk, gv, glog, gout), (qn, qr, ckv, kr) = project(hn, lp, pos)
    fox_o = fox_sample_attention(fq, fk, fv, flogf, k_past, v_past, logf_past)
    s5_o, s5_re, s5_im = s5_mixer(su, h_re, h_im, lp)
    gla_o, s_fin = gla_chunked(gq, gk, gv, glog, s_gla.astype(F32))
    mla_o = mla_sample_attention(qn, qr, ckv, kr, ckv_past, kr_past, lp['w_mla_uk'], lp['w_mla_uv'])
    mix = mix_output(fox_o, s5_o, gla_o, gout, mla_o, lp, x.dtype)
    return finish(x, mix, lp), (fk, fv, flogf, ckv, kr, s5_re, s5_im, s_fin)


def setup_inputs(seed: int = 0) -> dict:
    key = jax.random.key(seed)
    ks = iter(jax.random.split(key, 64))

    def nrm(shape, scale):
        return jax.random.normal(next(ks), shape, F32) * scale

    def gain(shape):
        return 1.0 + nrm(shape, 0.02)

    n_pages = PAST_LEN // PAGE_SIZE
    n_used = DEC_BATCH * n_pages
    n_pool = n_used + max(1, n_used // 4)
    page_table = jax.random.permutation(next(ks), n_pool)[:n_used].reshape(DEC_BATCH, n_pages).astype(jnp.int32)
    x_prompt = nrm((BATCH, SEQ, D_MODEL), 1.0)
    x_sample = nrm((DEC_BATCH, DEC_SEQ, D_MODEL), 1.0)
    cache_fox_k = nrm((DEPTH, n_pool, PAGE_SIZE, FOX_HEADS, FOX_DH), 1.0)
    cache_fox_v = nrm((DEPTH, n_pool, PAGE_SIZE, FOX_HEADS, FOX_DH), 1.0)
    cache_fox_logf = jax.nn.log_sigmoid(nrm((DEPTH, n_pool, PAGE_SIZE, FOX_HEADS), 1.0) + 3.0)
    cache_mla_ckv = nrm((DEPTH, n_pool, PAGE_SIZE, MLA_KV_RANK), 1.0)
    cache_mla_krope = nrm((DEPTH, n_pool, PAGE_SIZE, MLA_ROPE), 1.0)
    state_s5_re = nrm((DEPTH, DEC_BATCH, S5_GROUPS, S5_STATE), 0.1)
    state_s5_im = nrm((DEPTH, DEC_BATCH, S5_GROUPS, S5_STATE), 0.1)
    state_gla = nrm((DEPTH, DEC_BATCH, GLA_HEADS, GLA_DK, GLA_DV), 0.5)
    s5_a_re = -0.5 + nrm((DEPTH, S5_GROUPS, S5_STATE), 0.01)
    s5_a_im = jnp.pi * jnp.arange(S5_STATE, dtype=F32) + nrm((DEPTH, S5_GROUPS, S5_STATE), 0.01)
    s5_log_dt = jax.random.uniform(next(ks), (DEPTH, S5_GROUPS), F32, math.log(S5_DT_MIN), math.log(S5_DT_MAX))
    return {
        'x_prompt': x_prompt,
        'x_sample': x_sample,
        'cache_fox_k': cache_fox_k,
        'cache_fox_v': cache_fox_v,
        'cache_fox_logf': cache_fox_logf,
        'cache_mla_ckv': cache_mla_ckv,
        'cache_mla_krope': cache_mla_krope,
        'state_s5_re': state_s5_re,
        'state_s5_im': state_s5_im,
        'state_gla': state_gla,
        'page_table': page_table,
        'meta_tokens': nrm((N_META, D_MODEL), 1.0),
        'norm_mix_pre': gain((DEPTH, D_MODEL)),
        'norm_mix_post': gain((DEPTH, D_MODEL)),
        'norm_ffn_pre': gain((DEPTH, D_MODEL)),
        'norm_ffn_post': gain((DEPTH, D_MODEL)),
        'w_in': nrm((DEPTH, D_MODEL, IN_COLS), D_MODEL ** -0.5),
        'b_fox_f': nrm((DEPTH, FOX_HEADS), 0.1),
        's5_a_re': s5_a_re,
        's5_a_im': s5_a_im,
        's5_log_dt': s5_log_dt,
        's5_b_re': nrm((DEPTH, S5_GROUPS, S5_STATE, S5_GC), (2 * S5_GC) ** -0.5),
        's5_b_im': nrm((DEPTH, S5_GROUPS, S5_STATE, S5_GC), (2 * S5_GC) ** -0.5),
        's5_c_re': nrm((DEPTH, S5_GROUPS, S5_GC, S5_STATE), S5_STATE ** -0.5),
        's5_c_im': nrm((DEPTH, S5_GROUPS, S5_GC, S5_STATE), S5_STATE ** -0.5),
        's5_d': nrm((DEPTH, S5_WIDTH), 1.0),
        'w_s5_glu': nrm((DEPTH, S5_WIDTH, S5_WIDTH), S5_WIDTH ** -0.5),
        'b_s5_glu': nrm((DEPTH, S5_WIDTH), 0.02),
        'w_gla_gk': nrm((DEPTH, GLA_LOWRANK, GLA_HEADS * GLA_DK), GLA_LOWRANK ** -0.5),
        'b_gla_gk': nrm((DEPTH, GLA_HEADS * GLA_DK), 0.1),
        'gla_norm': gain((DEPTH, GLA_DV)),
        'mla_q_norm': gain((DEPTH, MLA_Q_RANK)),
        'w_mla_uq': nrm((DEPTH, MLA_Q_RANK, MLA_HEADS * (MLA_NOPE + MLA_ROPE)), MLA_Q_RANK ** -0.5),
        'mla_kv_norm': gain((DEPTH, MLA_KV_RANK)),
        'w_mla_uk': nrm((DEPTH, MLA_KV_RANK, MLA_HEADS, MLA_NOPE), MLA_KV_RANK ** -0.5),
        'w_mla_uv': nrm((DEPTH, MLA_KV_RANK, MLA_HEADS, MLA_V), MLA_KV_RANK ** -0.5),
        'w_out': nrm((DEPTH, MIX_WIDTH, D_MODEL), MIX_WIDTH ** -0.5),
        'w_ffn_gate': nrm((DEPTH, D_MODEL, D_FF), D_MODEL ** -0.5),
        'w_ffn_up': nrm((DEPTH, D_MODEL, D_FF), D_MODEL ** -0.5),
        'w_ffn_down': nrm((DEPTH, D_FF, D_MODEL), D_FF ** -0.5),
    }


def reference(x_prompt, x_sample, cache_fox_k, cache_fox_v, cache_fox_logf, cache_mla_ckv, cache_mla_krope,
              state_s5_re, state_s5_im, state_gla, page_table, meta_tokens,
              norm_mix_pre, norm_mix_post, norm_ffn_pre, norm_ffn_post, w_in, b_fox_f,
              s5_a_re, s5_a_im, s5_log_dt, s5_b_re, s5_b_im, s5_c_re, s5_c_im, s5_d, w_s5_glu, b_s5_glu,
              w_gla_gk, b_gla_gk, gla_norm, mla_q_norm, w_mla_uq, mla_kv_norm, w_mla_uk, w_mla_uv,
              w_out, w_ffn_gate, w_ffn_up, w_ffn_down):
    b_p = x_prompt.shape[0]
    meta = jnp.broadcast_to(meta_tokens.astype(x_prompt.dtype)[None], (b_p, N_META, D_MODEL))
    hp = jnp.concatenate([meta, x_prompt], axis=1)
    hs = x_sample
    prompt_rows, sample_rows = [], []
    for l in range(DEPTH):
        lp = dict(norm_mix_pre=norm_mix_pre[l], norm_mix_post=norm_mix_post[l],
                  norm_ffn_pre=norm_ffn_pre[l], norm_ffn_post=norm_ffn_post[l],
                  w_in=w_in[l], b_fox_f=b_fox_f[l],
                  s5_a_re=s5_a_re[l], s5_a_im=s5_a_im[l], s5_log_dt=s5_log_dt[l],
                  s5_b_re=s5_b_re[l], s5_b_im=s5_b_im[l], s5_c_re=s5_c_re[l], s5_c_im=s5_c_im[l],
                  s5_d=s5_d[l], w_s5_glu=w_s5_glu[l], b_s5_glu=b_s5_glu[l],
                  w_gla_gk=w_gla_gk[l], b_gla_gk=b_gla_gk[l], gla_norm=gla_norm[l],
                  mla_q_norm=mla_q_norm[l], w_mla_uq=w_mla_uq[l], mla_kv_norm=mla_kv_norm[l],
                  w_mla_uk=w_mla_uk[l], w_mla_uv=w_mla_uv[l], w_out=w_out[l],
                  w_ffn_gate=w_ffn_gate[l], w_ffn_up=w_ffn_up[l], w_ffn_down=w_ffn_down[l])
        hp, rows_p = layer_prompt(hp, lp)
        hs, rows_s = layer_sample(
            hs, lp,
            gather_pages(cache_fox_k[l], page_table), gather_pages(cache_fox_v[l], page_table),
            gather_pages(cache_fox_logf[l], page_table), gather_pages(cache_mla_ckv[l], page_table),
            gather_pages(cache_mla_krope[l], page_table),
            state_s5_re[l], state_s5_im[l], state_gla[l])
        prompt_rows.append(rows_p)
        sample_rows.append(rows_s)
    fk_p, fv_p, flf_p, ckv_p, kr_p, s5re_p, s5im_p, gla_p = [
        jnp.stack(r).astype(x_prompt.dtype) for r in zip(*prompt_rows)]
    fk_s, fv_s, flf_s, ckv_s, kr_s, s5re_s, s5im_s, gla_s = [
        jnp.stack(r).astype(x_sample.dtype) for r in zip(*sample_rows)]
    y_prompt = hp[:, N_META:]
    y_sample = hs
    return (y_prompt, y_sample, fk_p, fk_s, fv_p, fv_s, flf_p, flf_s, ckv_p, ckv_s, kr_p, kr_s,
            s5re_p, s5re_s, s5im_p, s5im_s, gla_p, gla_s)
```

```python
import functools
import math

import jax
import jax.numpy as jnp
import numpy as np
from jax import lax
from jax.experimental import pallas as pl
from jax.experimental.pallas import tpu as pltpu

F32 = jnp.float32
BF16 = jnp.bfloat16
HIGHEST = lax.Precision.HIGHEST

N_META = 16
HEAD_DIM = 64
NORM_EPS = 1e-6
N_HEADS = 4
GROUP_WIDTH = N_HEADS * HEAD_DIM
S5_GC = 16
S5_GROUPS = GROUP_WIDTH // S5_GC
S5_STATE = 64
S5_LANES = S5_GROUPS * S5_STATE
GLA_LOWRANK = 16
GLA_GATE_NORM = 16.0
MLA_ROPE = HEAD_DIM // 2
MLA_Q_RANK = 3 * GROUP_WIDTH // 4
MLA_KV_RANK = GROUP_WIDTH // 2
FOX_SCALE = HEAD_DIM ** -0.5
GLA_SCALE = HEAD_DIM ** -0.5
MLA_SCALE = (HEAD_DIM + MLA_ROPE) ** -0.5
ROPE_THETA = 10000.0
MLA_SLOT = 128

SUBLANES = 8
LANES = 128
VMEM_LIMIT_BYTES = 56 * 1024 * 1024

PROJ_TILE = 688
ATTN_TILE = 344
S5_CHUNK = 48
GLA_CHUNK = 48
FFN_TILE = 344
DECODE_PAGES_PER_STEP = 8

C_FOX = 0
C_S5 = 768
C_GLA = 1024
C_MKV = 2048
C_SMALL = 2176
C_MQ = 2304
N_ZCOLS = 2560


def _pick_tile(n, target):
    best = None
    for t in range(SUBLANES, min(n, target) + 1, SUBLANES):
        if n % t == 0:
            best = t
    return best if best is not None else n


def _rms(x, w):
    return x * lax.rsqrt(jnp.mean(x * x, axis=-1, keepdims=True) + NORM_EPS) * w


def _log_sigmoid(x):
    return jnp.minimum(x, 0.0) - jnp.log1p(jnp.exp(-jnp.abs(x)))


def _sigmoid(x):
    return 1.0 / (1.0 + jnp.exp(-x))


def _dot(a, b):
    return jnp.dot(a, b, preferred_element_type=F32)


def _dot_nt(a, b):
    return lax.dot_general(a, b, (((1,), (1,)), ((), ())), preferred_element_type=F32)


def _dot_tn(a, b):
    return lax.dot_general(a, b, (((0,), (0,)), ((), ())), preferred_element_type=F32)


def _const_spec(shape):
    nd = len(shape)
    return pl.BlockSpec(shape, lambda *_: (0,) * nd)


def _proj_kernel(x_ref, nw_ref, w_ref, bff_ref, wgk_ref, bgk_ref, qnw_ref, wuqa_ref, wuqb_ref,
                 kvnw_ref, wuk_ref, wuv_ref, pkr_ref, tqc_ref, tqs_ref, tkr_ref,
                 fq_ref, fk_ref, fv_ref, fkb_ref, fvb_ref, logf_ref, c_ref, su_ref, gla_ref,
                 qcat_ref, kcat_ref, vmla_ref, ckv_ref, kr_ref, carry_ref, *, cumsum):
    tm = x_ref.shape[0]
    hn = _rms(x_ref[...], nw_ref[...]).astype(BF16)

    zf = _dot(hn, w_ref[:, C_FOX:C_FOX + 3 * GROUP_WIDTH])
    fq_ref[...] = (zf[:, :GROUP_WIDTH] * FOX_SCALE).astype(BF16)
    fk = zf[:, GROUP_WIDTH:2 * GROUP_WIDTH]
    fv = zf[:, 2 * GROUP_WIDTH:]
    fk_ref[...] = fk
    fv_ref[...] = fv
    fkb_ref[...] = fk.astype(BF16)
    fvb_ref[...] = fv.astype(BF16)

    su_ref[...] = _dot(hn, w_ref[:, C_S5:C_S5 + GROUP_WIDTH])

    zs = _dot(hn, w_ref[:, C_SMALL:C_SMALL + LANES])
    mkr = zs[:, 0:MLA_ROPE]
    mkr_sw = zs[:, MLA_ROPE:2 * MLA_ROPE]
    glr = zs[:, 2 * MLA_ROPE:2 * MLA_ROPE + GLA_LOWRANK]
    ff = zs[:, 2 * MLA_ROPE + GLA_LOWRANK:2 * MLA_ROPE + GLA_LOWRANK + N_HEADS]

    logf = _log_sigmoid(ff + bff_ref[...])
    logf_ref[...] = logf
    if cumsum:
        ti = pl.program_id(1)

        @pl.when(ti == 0)
        def _():
            carry_ref[...] = jnp.zeros_like(carry_ref)

        row = lax.broadcasted_iota(jnp.int32, (tm, tm), 0)
        col = lax.broadcasted_iota(jnp.int32, (tm, tm), 1)
        tri = (row >= col).astype(F32)
        c = jnp.dot(tri, logf, precision=HIGHEST, preferred_element_type=F32) + carry_ref[...]
        c_ref[...] = c
        carry_ref[...] = c[tm - 1:tm, :]
    else:
        c_ref[...] = logf

    tkr = tkr_ref[...]
    kr = mkr * tkr[:, :MLA_ROPE] + mkr_sw * tkr[:, MLA_ROPE:]
    kr_ref[...] = kr

    glog = _log_sigmoid(_dot(glr.astype(BF16), wgk_ref[...]) + bgk_ref[...]) * (1.0 / GLA_GATE_NORM)
    zg = _dot(hn, w_ref[:, C_GLA:C_GLA + 4 * GROUP_WIDTH])
    gla_ref[:, 0:GROUP_WIDTH] = zg[:, 0:GROUP_WIDTH] * GLA_SCALE
    gla_ref[:, GROUP_WIDTH:3 * GROUP_WIDTH] = zg[:, GROUP_WIDTH:3 * GROUP_WIDTH]
    gla_ref[:, 3 * GROUP_WIDTH:4 * GROUP_WIDTH] = glog
    gla_ref[:, 4 * GROUP_WIDTH:5 * GROUP_WIDTH] = zg[:, 3 * GROUP_WIDTH:]

    mkv = _dot(hn, w_ref[:, C_MKV:C_MKV + MLA_KV_RANK])
    ckv = _rms(mkv, kvnw_ref[...])
    ckv_ref[...] = ckv
    ckv_b = ckv.astype(BF16)
    kcat = _dot(ckv_b, wuk_ref[...]) + _dot(kr.astype(BF16), pkr_ref[...])
    kcat_ref[...] = kcat.astype(BF16)
    vmla_ref[...] = _dot(ckv_b, wuv_ref[...]).astype(BF16)

    mq = _dot(hn, w_ref[:, C_MQ:C_MQ + MLA_Q_RANK])
    qln = _rms(mq, qnw_ref[...]).astype(BF16)
    qcat = _dot(qln, wuqa_ref[...]) * tqc_ref[...] + _dot(qln, wuqb_ref[...]) * tqs_ref[...]
    qcat_ref[...] = (qcat * MLA_SCALE).astype(BF16)


def _proj_call(x3, lw, tabs, *, cumsum):
    nb, t, d = x3.shape
    tm = _pick_tile(t, PROJ_TILE)
    nt = t // tm
    tqc, tqs, tkr = tabs
    gw = GROUP_WIDTH
    qs = N_HEADS * MLA_SLOT

    def tok(width):
        return pl.BlockSpec((None, tm, width), lambda b, i: (b, i, 0))

    def tab(width):
        return pl.BlockSpec((tm, width), lambda b, i: (i, 0))

    consts = [lw['norm_mix_pre'], lw['wcat'], lw['b_fox_f'], lw['w_gla_gk'], lw['b_gla_gk'],
              lw['mla_q_norm'], lw['w_uq_a'], lw['w_uq_b'], lw['mla_kv_norm'], lw['w_uk_slots'],
              lw['w_uv'], lw['p_kr']]
    in_specs = [tok(d)] + [_const_spec(c.shape) for c in consts] + [tab(qs), tab(qs), tab(2 * MLA_ROPE)]
    out_shape = [
        jax.ShapeDtypeStruct((nb, t, gw), BF16),
        jax.ShapeDtypeStruct((nb, t, gw), F32),
        jax.ShapeDtypeStruct((nb, t, gw), F32),
        jax.ShapeDtypeStruct((nb, t, gw), BF16),
        jax.ShapeDtypeStruct((nb, t, gw), BF16),
        jax.ShapeDtypeStruct((nb, t, N_HEADS), F32),
        jax.ShapeDtypeStruct((nb, t, N_HEADS), F32),
        jax.ShapeDtypeStruct((t, nb * gw), F32),
        jax.ShapeDtypeStruct((nb, t, 5 * gw), F32),
        jax.ShapeDtypeStruct((nb, t, qs), BF16),
        jax.ShapeDtypeStruct((nb, t, qs), BF16),
        jax.ShapeDtypeStruct((nb, t, gw), BF16),
        jax.ShapeDtypeStruct((nb, t, MLA_KV_RANK), F32),
        jax.ShapeDtypeStruct((nb, t, MLA_ROPE), F32),
    ]
    out_specs = [tok(gw), tok(gw), tok(gw), tok(gw), tok(gw), tok(N_HEADS), tok(N_HEADS),
                 pl.BlockSpec((tm, gw), lambda b, i: (i, b)),
                 tok(5 * gw), tok(qs), tok(qs), tok(gw), tok(MLA_KV_RANK), tok(MLA_ROPE)]
    return pl.pallas_call(
        functools.partial(_proj_kernel, cumsum=cumsum),
        out_shape=out_shape,
        grid=(nb, nt),
        in_specs=in_specs,
        out_specs=out_specs,
        scratch_shapes=[pltpu.VMEM((1, N_HEADS), F32)],
        compiler_params=pltpu.CompilerParams(
            dimension_semantics=("arbitrary", "arbitrary"), vmem_limit_bytes=VMEM_LIMIT_BYTES),
        name="proj",
    )(x3, *consts, tqc, tqs, tkr)


def _attn_kernel(*refs, dk, has_bias):
    if has_bias:
        q_ref, k_ref, v_ref, c_ref, o_ref, crow_ref = refs
    else:
        q_ref, k_ref, v_ref, o_ref = refs
    tq = q_ref.shape[0]
    t = k_ref.shape[0]
    nk = t // tq
    qi = pl.program_id(1)

    if has_bias:
        @pl.when(qi == 0)
        def _():
            eye = (lax.broadcasted_iota(jnp.int32, (SUBLANES, N_HEADS), 0)
                   == lax.broadcasted_iota(jnp.int32, (SUBLANES, N_HEADS), 1)).astype(F32)
            for j in range(nk):
                crow_ref[j] = lax.dot_general(eye, c_ref[j * tq:(j + 1) * tq, :],
                                              (((1,), (1,)), ((), ())), precision=HIGHEST,
                                              preferred_element_type=F32)

    q0 = pl.multiple_of(qi * tq, tq)
    causal = (lax.broadcasted_iota(jnp.int32, (tq, tq), 0)
              >= lax.broadcasted_iota(jnp.int32, (tq, tq), 1))

    for h in range(N_HEADS):
        q = q_ref[:, h * dk:(h + 1) * dk]
        if has_bias:
            cq = c_ref[pl.ds(q0, tq), h:h + 1]

        def scores(k0, kj):
            s = _dot_nt(q, k_ref[pl.ds(k0, tq), h * dk:(h + 1) * dk])
            if has_bias:
                s = s + (cq - crow_ref[kj, h:h + 1, :])
            return s

        def update(carry, s, k0):
            m, l, acc = carry
            m_new = jnp.maximum(m, jnp.max(s, axis=-1, keepdims=True))
            alpha = jnp.exp(m - m_new)
            p = jnp.exp(s - m_new)
            l = alpha * l + jnp.sum(p, axis=-1, keepdims=True)
            v = v_ref[pl.ds(k0, tq), h * HEAD_DIM:(h + 1) * HEAD_DIM]
            acc = alpha * acc + _dot(p.astype(BF16), v)
            return m_new, l, acc

        def body(kj, carry):
            k0 = pl.multiple_of(kj * tq, tq)
            return update(carry, scores(k0, kj), k0)

        init = (jnp.full((tq, 1), -jnp.inf, F32), jnp.zeros((tq, 1), F32),
                jnp.zeros((tq, HEAD_DIM), F32))
        carry = lax.fori_loop(0, qi, body, init)
        s = jnp.where(causal, scores(q0, qi), -jnp.inf)
        m, l, acc = update(carry, s, q0)
        o_ref[:, h * HEAD_DIM:(h + 1) * HEAD_DIM] = acc / l


def _attn_call(q, k, v, c=None):
    nb, t, qw = q.shape
    dk = qw // N_HEADS
    tq = _pick_tile(t, ATTN_TILE)
    nq = t // tq
    has_bias = c is not None
    in_specs = [pl.BlockSpec((None, tq, qw), lambda b, i: (b, i, 0)),
                pl.BlockSpec((None, t, qw), lambda b, i: (b, 0, 0)),
                pl.BlockSpec((None, t, GROUP_WIDTH), lambda b, i: (b, 0, 0))]
    args = [q, k, v]
    scratch = []
    if has_bias:
        in_specs.append(pl.BlockSpec((None, t, N_HEADS), lambda b, i: (b, 0, 0)))
        args.append(c)
        scratch.append(pltpu.VMEM((nq, SUBLANES, tq), F32))
    return pl.pallas_call(
        functools.partial(_attn_kernel, dk=dk, has_bias=has_bias),
        out_shape=jax.ShapeDtypeStruct((nb, t, GROUP_WIDTH), F32),
        grid=(nb, nq),
        in_specs=in_specs,
        out_specs=pl.BlockSpec((None, tq, GROUP_WIDTH), lambda b, i: (b, i, 0)),
        scratch_shapes=scratch,
        compiler_params=pltpu.CompilerParams(
            dimension_semantics=("arbitrary", "arbitrary"), vmem_limit_bytes=VMEM_LIMIT_BYTES),
        name="fox_attn" if has_bias else "mla_attn",
    )(*args)


def _s5_disc_kernel(are_ref, aim_ref, ldt_ref, bre_ref, bim_ref,
                    abr_ref, abi_ref, bbr_ref, bbi_ref):
    dt = jnp.exp(ldt_ref[...])
    ar = are_ref[...]
    ai = aim_ref[...]
    mag = jnp.exp(ar * dt)
    abr = mag * jnp.cos(ai * dt)
    abi = mag * jnp.sin(ai * dt)
    nr = abr - 1.0
    ni = abi
    den = ar * ar + ai * ai
    fr = (nr * ar + ni * ai) / den
    fi = (ni * ar - nr * ai) / den
    abr_ref[...] = abr
    abi_ref[...] = abi
    for c in range(S5_GC):
        br = bre_ref[c]
        bi = bim_ref[c]
        bbr_ref[c] = fr * br - fi * bi
        bbi_ref[c] = fr * bi + fi * br


def _s5_discretize(a_re, a_im, log_dt, b_re, b_im):
    g, p = a_re.shape
    bt = lambda b: jnp.transpose(b, (2, 0, 1))
    outs = pl.pallas_call(
        _s5_disc_kernel,
        out_shape=[jax.ShapeDtypeStruct((g, p), F32), jax.ShapeDtypeStruct((g, p), F32),
                   jax.ShapeDtypeStruct((S5_GC, g, p), F32), jax.ShapeDtypeStruct((S5_GC, g, p), F32)],
        name="s5_discretize",
    )(a_re, a_im, log_dt.reshape(g, 1), bt(b_re), bt(b_im))
    abr, abi, bbr, bbi = outs
    eye = jnp.eye(g, dtype=F32)

    def in_blockdiag(bb):
        m = jnp.transpose(bb, (1, 0, 2))[:, :, None, :] * eye[:, None, :, None]
        return m.reshape(g * S5_GC, g * p).astype(BF16)

    return abr.reshape(1, g * p), abi.reshape(1, g * p), in_blockdiag(bbr), in_blockdiag(bbi)


def _s5_out_blockdiag(c):
    g = c.shape[0]
    eye = jnp.eye(g, dtype=F32)
    m = jnp.transpose(c, (0, 2, 1))[:, :, None, :] * eye[:, None, :, None]
    return m.reshape(g * c.shape[2], g * c.shape[1]).astype(BF16)


def _gelu_tanh(x):
    return 0.5 * x * (1.0 + jnp.tanh(math.sqrt(2.0 / math.pi) * (x + 0.044715 * (x * x * x))))


def _s5_kernel(u_ref, h0r_ref, h0i_ref, ar_ref, ai_ref, bre_ref, bim_ref, cre_ref, cim_ref,
               d_ref, wglu_ref, bglu_ref, o_ref, hr_ref, hi_ref, sr_ref, si_ref, *, nb):
    i = pl.program_id(0)
    tc = u_ref.shape[0] // nb

    @pl.when(i == 0)
    def _():
        hr_ref[...] = h0r_ref[...]
        hi_ref[...] = h0i_ref[...]

    u = u_ref[...]
    ub = u.astype(BF16)
    sr_ref[...] = _dot(ub, bre_ref[...])
    si_ref[...] = _dot(ub, bim_ref[...])
    ar = ar_ref[...]
    ai = ai_ref[...]

    def step(t, carry):
        hr, hi = carry
        r0 = pl.multiple_of(t * nb, nb)
        nhr = ar * hr - ai * hi + sr_ref[pl.ds(r0, nb), :]
        nhi = ar * hi + ai * hr + si_ref[pl.ds(r0, nb), :]
        sr_ref[pl.ds(r0, nb), :] = nhr
        si_ref[pl.ds(r0, nb), :] = nhi
        return nhr, nhi

    hr, hi = lax.fori_loop(0, tc, step, (hr_ref[...], hi_ref[...]))
    hr_ref[...] = hr
    hi_ref[...] = hi

    y = (_dot(sr_ref[...].astype(BF16), cre_ref[...]) - _dot(si_ref[...].astype(BF16), cim_ref[...])
         + d_ref[...] * u)
    g = _gelu_tanh(y)
    o_ref[...] = g * _sigmoid(_dot(g.astype(BF16), wglu_ref[...]) + bglu_ref[...])


def _s5_call(u_tm, h0r, h0i, lw, nb):
    rows = u_tm.shape[0]
    t = rows // nb
    tc = _pick_tile(t, S5_CHUNK) if t >= SUBLANES else t
    r = tc * nb
    consts = [lw['s5_abar_re'], lw['s5_abar_im'], lw['s5_bin_re'], lw['s5_bin_im'],
              lw['s5_cout_re'], lw['s5_cout_im'], lw['s5_d'], lw['w_s5_glu'], lw['b_s5_glu']]
    state_spec = _const_spec((nb, S5_LANES))
    return pl.pallas_call(
        functools.partial(_s5_kernel, nb=nb),
        out_shape=[jax.ShapeDtypeStruct((rows, GROUP_WIDTH), F32),
                   jax.ShapeDtypeStruct((nb, S5_LANES), F32),
                   jax.ShapeDtypeStruct((nb, S5_LANES), F32)],
        grid=(t // tc,),
        in_specs=[pl.BlockSpec((r, GROUP_WIDTH), lambda i: (i, 0)), state_spec, state_spec]
                 + [_const_spec(c.shape) for c in consts],
        out_specs=[pl.BlockSpec((r, GROUP_WIDTH), lambda i: (i, 0)), state_spec, state_spec],
        scratch_shapes=[pltpu.VMEM((r, S5_LANES), F32), pltpu.VMEM((r, S5_LANES), F32)],
        compiler_params=pltpu.CompilerParams(
            dimension_semantics=("arbitrary",), vmem_limit_bytes=VMEM_LIMIT_BYTES),
        name="s5_scan",
    )(u_tm, h0r, h0i, *consts)


def _gla_kernel(x_ref, s0_ref, nw_ref, o_ref, st_ref):
    ci = pl.program_id(1)
    c = x_ref.shape[0]
    gw = GROUP_WIDTH

    @pl.when(ci == 0)
    def _():
        st_ref[...] = s0_ref[...]

    q = x_ref[:, 0:gw]
    k = x_ref[:, gw:2 * gw]
    v = x_ref[:, 2 * gw:3 * gw]
    g = x_ref[:, 3 * gw:4 * gw]
    gate = x_ref[:, 4 * gw:5 * gw]

    row = lax.broadcasted_iota(jnp.int32, (c, c), 0)
    col = lax.broadcasted_iota(jnp.int32, (c, c), 1)
    causal = row >= col
    bc = jnp.dot(causal.astype(F32), g, precision=HIGHEST, preferred_element_type=F32)
    b_last = bc[c - 1:c, :]
    b_mid = bc[c // 2:c // 2 + 1, :]
    q_in = (q * jnp.exp(bc - b_mid)).astype(BF16)
    k_in = (k * jnp.exp(b_mid - bc)).astype(BF16)
    q_st = (q * jnp.exp(bc)).astype(BF16)
    k_st = (k * jnp.exp(b_last - bc)).astype(BF16)
    vb = v.astype(BF16)
    decay = jnp.exp(b_last)

    for h in range(N_HEADS):
        sl = slice(h * HEAD_DIM, (h + 1) * HEAD_DIM)
        att = jnp.where(causal, _dot_nt(q_in[:, sl], k_in[:, sl]), 0.0)
        st = st_ref[sl, :]
        o = _dot_nt(q_st[:, sl], st.astype(BF16)) + _dot(att.astype(BF16), vb[:, sl])
        st_ref[sl, :] = st * decay[:, sl] + _dot_tn(vb[:, sl], k_st[:, sl])
        o = _rms(o, nw_ref[...])
        gt = gate[:, sl]
        o_ref[:, sl] = o * (gt * _sigmoid(gt))


def _gla_call(gla_in, s0t, norm_w):
    nb, t, w = gla_in.shape
    c = _pick_tile(t, GLA_CHUNK)
    return pl.pallas_call(
        _gla_kernel,
        out_shape=[jax.ShapeDtypeStruct((nb, t, GROUP_WIDTH), F32),
                   jax.ShapeDtypeStruct((nb, GROUP_WIDTH, HEAD_DIM), F32)],
        grid=(nb, t // c),
        in_specs=[pl.BlockSpec((None, c, w), lambda b, i: (b, i, 0)),
                  pl.BlockSpec((None, GROUP_WIDTH, HEAD_DIM), lambda b, i: (b, 0, 0)),
                  _const_spec(norm_w.shape)],
        out_specs=[pl.BlockSpec((None, c, GROUP_WIDTH), lambda b, i: (b, i, 0)),
                   pl.BlockSpec((None, GROUP_WIDTH, HEAD_DIM), lambda b, i: (b, 0, 0))],
        compiler_params=pltpu.CompilerParams(
            dimension_semantics=("arbitrary", "arbitrary"), vmem_limit_bytes=VMEM_LIMIT_BYTES),
        name="gla",
    )(gla_in, s0t, norm_w)


def _finish_kernel(x_ref, fox_ref, s5_ref, gla_ref, mla_ref, wout_ref, npost_ref, npre_ref,
                   nffn_ref, wg_ref, wu_ref, wd_ref, o_ref):
    gw = GROUP_WIDTH
    mix = (_dot(fox_ref[...].astype(BF16), wout_ref[0:gw, :])
           + _dot(s5_ref[...].astype(BF16), wout_ref[gw:2 * gw, :])
           + _dot(gla_ref[...].astype(BF16), wout_ref[2 * gw:3 * gw, :])
           + _dot(mla_ref[...].astype(BF16), wout_ref[3 * gw:4 * gw, :]))
    x1 = x_ref[...] + _rms(mix, npost_ref[...])
    hf = _rms(x1, npre_ref[...]).astype(BF16)
    gate = _dot(hf, wg_ref[...])
    act = (gate * _sigmoid(gate) * _dot(hf, wu_ref[...])).astype(BF16)
    f = _dot(act, wd_ref[...])
    o_ref[...] = x1 + _rms(f, nffn_ref[...])


def _finish_call(x3, fox_o, s5_tm, gla_o, mla_o, lw):
    nb, t, d = x3.shape
    tm = _pick_tile(t, FFN_TILE)
    gw = GROUP_WIDTH

    def tok(width):
        return pl.BlockSpec((None, tm, width), lambda b, i: (b, i, 0))

    consts = [lw['w_out'], lw['norm_mix_post'], lw['norm_ffn_pre'], lw['norm_ffn_post'],
              lw['w_ffn_gate'], lw['w_ffn_up'], lw['w_ffn_down']]
    const_specs = [pl.BlockSpec(c.shape, lambda b, i: (0, 0), pipeline_mode=pl.Buffered(1))
                   for c in consts]
    return pl.pallas_call(
        _finish_kernel,
        out_shape=jax.ShapeDtypeStruct((nb, t, d), F32),
        grid=(nb, t // tm),
        in_specs=[tok(d), tok(gw), pl.BlockSpec((tm, gw), lambda b, i: (i, b)), tok(gw), tok(gw)]
                 + const_specs,
        out_specs=tok(d),
        compiler_params=pltpu.CompilerParams(
            dimension_semantics=("arbitrary", "arbitrary"), vmem_limit_bytes=VMEM_LIMIT_BYTES),
        name="mix_ffn",
    )(x3, fox_o, s5_tm, gla_o, mla_o, *consts)


def _fox_bias_kernel(pt_ref, logf_hbm, o_ref, buf_ref, sem_ref, mcross_ref):
    b = pl.program_id(0)
    nb = pl.num_programs(0)
    n_pages = buf_ref.shape[1]
    page = buf_ref.shape[3]
    rows = n_pages * SUBLANES

    def page_copy(bb, slot, p):
        return pltpu.make_async_copy(logf_hbm.at[pt_ref[bb, p]], buf_ref.at[slot, p], sem_ref.at[slot])

    def start_all(bb, slot):
        def go(p, _):
            page_copy(bb, slot, p).start()
            return 0
        lax.fori_loop(0, n_pages, go, 0)

    @pl.when(b == 0)
    def _():
        r = lax.broadcasted_iota(jnp.int32, (rows, rows), 0)
        c = lax.broadcasted_iota(jnp.int32, (rows, rows), 1)
        mcross_ref[...] = jnp.where((r % SUBLANES == c % SUBLANES) & (c // SUBLANES > r // SUBLANES),
                                    1.0, 0.0).astype(F32)
        start_all(0, 0)

    slot = b % 2

    @pl.when(b + 1 < nb)
    def _():
        start_all(b + 1, 1 - slot)

    def wait_one(p, _):
        page_copy(b, slot, p).wait()
        return 0
    lax.fori_loop(0, n_pages, wait_one, 0)

    x = buf_ref[slot].reshape(rows, page)
    j = lax.broadcasted_iota(jnp.int32, (page, page), 0)
    s = lax.broadcasted_iota(jnp.int32, (page, page), 1)
    later = (j > s).astype(F32)
    within = jnp.dot(x, later, precision=HIGHEST, preferred_element_type=F32)
    total = jnp.dot(x, jnp.ones((page, page), F32), precision=HIGHEST, preferred_element_type=F32)
    cross = jnp.dot(mcross_ref[...], total, precision=HIGHEST, preferred_element_type=F32)
    o_ref[...] = (within + cross).reshape(n_pages, SUBLANES, page)


def _fox_bias_call(page_table, logf_t):
    nb, n_pages = page_table.shape
    page = logf_t.shape[2]
    rows = n_pages * SUBLANES
    return pl.pallas_call(
        _fox_bias_kernel,
        out_shape=jax.ShapeDtypeStruct((nb, n_pages, SUBLANES, page), F32),
        grid_spec=pltpu.PrefetchScalarGridSpec(
            num_scalar_prefetch=1,
            grid=(nb,),
            in_specs=[pl.BlockSpec(memory_space=pl.ANY)],
            out_specs=pl.BlockSpec((None, n_pages, SUBLANES, page), lambda b, pt: (b, 0, 0, 0)),
            scratch_shapes=[pltpu.VMEM((2, n_pages, SUBLANES, page), F32),
                            pltpu.SemaphoreType.DMA((2,)),
                            pltpu.VMEM((rows, rows), F32)]),
        compiler_params=pltpu.CompilerParams(
            dimension_semantics=("arbitrary",), vmem_limit_bytes=VMEM_LIMIT_BYTES),
        name="fox_decode_bias",
    )(page_table, logf_t)


def _decode_kernel(pt_ref, bias_ref, fq_ref, qcat_ref, cnew_ref, knew_ref, vnew_ref, ckvnew_ref,
                   krnew_ref, wuk_ref, wuv_ref, k_hbm, v_hbm, ckv_hbm, kr_hbm,
                   fox_ref, mla_ref,
                   kbuf, vbuf, cbuf, rbuf, sem_ref, qabs_ref, olat_ref, *, g_pages):
    b = pl.program_id(0)
    nb = pl.num_programs(0)
    n_pages = bias_ref.shape[0]
    nc = n_pages // g_pages
    page = bias_ref.shape[2]
    gw = GROUP_WIDTH

    def copies(bb, ci, slot):
        out = []
        for j in range(g_pages):
            pid = pt_ref[bb, ci * g_pages + j]
            out.append(pltpu.make_async_copy(k_hbm.at[pid], kbuf.at[slot, j], sem_ref.at[0, slot]))
            out.append(pltpu.make_async_copy(v_hbm.at[pid], vbuf.at[slot, j], sem_ref.at[1, slot]))
            out.append(pltpu.make_async_copy(ckv_hbm.at[pid], cbuf.at[slot, j], sem_ref.at[2, slot]))
            out.append(pltpu.make_async_copy(kr_hbm.at[pid], rbuf.at[slot, j], sem_ref.at[3, slot]))
        return out

    def start(bb, ci, slot):
        for cp in copies(bb, ci, slot):
            cp.start()

    @pl.when(b == 0)
    def _():
        start(0, 0, 0)
        for h in range(N_HEADS):
            qn = qcat_ref[:, h * MLA_SLOT:h * MLA_SLOT + HEAD_DIM].astype(BF16)
            qabs_ref[h] = _dot_nt(qn, wuk_ref[h])

    row8 = lax.broadcasted_iota(jnp.int32, (SUBLANES, gw), 0)
    lane8 = lax.broadcasted_iota(jnp.int32, (SUBLANES, gw), 1)
    head_mask = row8 == lane8 // HEAD_DIM
    qf8 = jnp.where(head_mask, jnp.broadcast_to(fq_ref[pl.ds(b, 1), :], (SUBLANES, gw)),
                    0.0).astype(BF16)
    rowk = lax.broadcasted_iota(jnp.int32, (SUBLANES, MLA_KV_RANK), 0)
    qabs8 = jnp.zeros((SUBLANES, MLA_KV_RANK), F32)
    rowr = lax.broadcasted_iota(jnp.int32, (SUBLANES, MLA_ROPE), 0)
    qr8 = jnp.zeros((SUBLANES, MLA_ROPE), F32)
    qrow = qcat_ref[pl.ds(b, 1), :]
    for h in range(N_HEADS):
        qabs8 = jnp.where(rowk == N_HEADS + h,
                          jnp.broadcast_to(qabs_ref[h, pl.ds(b, 1), :], (SUBLANES, MLA_KV_RANK)), qabs8)
        qr_h = qrow[:, h * MLA_SLOT + HEAD_DIM:h * MLA_SLOT + HEAD_DIM + MLA_ROPE]
        qr8 = jnp.where(rowr == N_HEADS + h, jnp.broadcast_to(qr_h, (SUBLANES, MLA_ROPE)), qr8)
    qabs8 = qabs8.astype(BF16)
    qr8 = qr8.astype(BF16)
    cnew =cnew_ref[pl.ds(b, 1), :]
    eye8 = (lax.broadcasted_iota(jnp.int32, (SUBLANES, SUBLANES), 0)
            == lax.broadcasted_iota(jnp.int32, (SUBLANES, SUBLANES), 1))
    cnew_col = jnp.sum(jnp.where(eye8, jnp.broadcast_to(cnew, (SUBLANES, SUBLANES)), 0.0),
                       axis=-1, keepdims=True)

    def chunk(ci, carry):
        m, l, acc_f, acc_m = carry
        gidx = b * nc + ci
        slot = gidx % 2

        @pl.when(ci + 1 < nc)
        def _():
            start(b, ci + 1, 1 - slot)

        @pl.when((ci + 1 == nc) & (b + 1 < nb))
        def _():
            start(b + 1, 0, 1 - slot)

        for cp in copies(b, ci, slot):
            cp.wait()

        s_list = []
        for j in range(g_pages):
            kp = kbuf[slot, j].astype(BF16)
            cp_ = cbuf[slot, j].astype(BF16)
            rp = rbuf[slot, j].astype(BF16)
            s = _dot_nt(qf8, kp) + _dot_nt(qabs8, cp_) + _dot_nt(qr8, rp)
            s_list.append(s + bias_ref[ci * g_pages + j] + cnew_col)
        s = jnp.concatenate(s_list, axis=-1)
        m_new = jnp.maximum(m, jnp.max(s, axis=-1, keepdims=True))
        alpha = jnp.exp(m - m_new)
        p = jnp.exp(s - m_new)
        l = alpha * l + jnp.sum(p, axis=-1, keepdims=True)
        pb = p.astype(BF16)
        of = jnp.zeros((SUBLANES, gw), F32)
        om = jnp.zeros((SUBLANES, MLA_KV_RANK), F32)
        for j in range(g_pages):
            pj = pb[:, j * page:(j + 1) * page]
            of = of + _dot(pj, vbuf[slot, j].astype(BF16))
            om = om + _dot(pj, cbuf[slot, j].astype(BF16))
        return m_new, l, alpha * acc_f + of, alpha * acc_m + om

    init = (jnp.full((SUBLANES, 1), -jnp.inf, F32), jnp.zeros((SUBLANES, 1), F32),
            jnp.zeros((SUBLANES, gw), F32), jnp.zeros((SUBLANES, MLA_KV_RANK), F32))
    m, l, acc_f, acc_m = lax.fori_loop(0, nc, chunk, init)

    knew = knew_ref[pl.ds(b, 1), :].astype(BF16).astype(F32)
    vnew = vnew_ref[pl.ds(b, 1), :].astype(BF16).astype(F32)
    ckvnew = ckvnew_ref[pl.ds(b, 1), :].astype(BF16).astype(F32)
    krnew = krnew_ref[pl.ds(b, 1), :].astype(BF16).astype(F32)
    s_new = (jnp.sum(qf8.astype(F32) * knew, axis=-1, keepdims=True)
             + jnp.sum(qabs8.astype(F32) * ckvnew, axis=-1, keepdims=True)
             + jnp.sum(qr8.astype(F32) * krnew, axis=-1, keepdims=True))
    m_fin = jnp.maximum(m, s_new)
    alpha = jnp.exp(m - m_fin)
    p_new = jnp.exp(s_new - m_fin)
    l = alpha * l + p_new
    p_new = p_new.astype(BF16).astype(F32)
    acc_f = alpha * acc_f + p_new * vnew
    acc_m = alpha * acc_m + p_new * ckvnew
    inv = 1.0 / l
    fox_ref[pl.ds(b, 1), :] = jnp.sum(jnp.where(head_mask, acc_f * inv, 0.0), axis=0, keepdims=True)
    olat = acc_m * inv
    for h in range(N_HEADS):
        olat_ref[h, pl.ds(b, 1), :] = olat[N_HEADS + h:N_HEADS + h + 1, :]

    @pl.when(b == nb - 1)
    def _():
        for h in range(N_HEADS):
            mla_ref[:, h * HEAD_DIM:(h + 1) * HEAD_DIM] = _dot(olat_ref[h].astype(BF16), wuv_ref[h])


def _decode_call(page_table, bias, fq, qcat, cnew8, knew, vnew, ckvnew, krnew, wuk_h, wuv_h,
                 k_pool, v_pool, ckv_pool, kr_pool):
    nb, n_pages = page_table.shape
    page = k_pool.shape[1]
    g_pages = math.gcd(n_pages, DECODE_PAGES_PER_STEP)
    gw = GROUP_WIDTH
    vm = [fq, qcat, cnew8, knew, vnew, ckvnew, krnew, wuk_h, wuv_h]
    in_specs = ([pl.BlockSpec((None, n_pages, SUBLANES, page), lambda b, pt: (b, 0, 0, 0))]
                + [pl.BlockSpec(a.shape, lambda b, pt, _n=a.ndim: (0,) * _n) for a in vm]
                + [pl.BlockSpec(memory_space=pl.ANY)] * 4)
    out_full = pl.BlockSpec((nb, gw), lambda b, pt: (0, 0))
    return pl.pallas_call(
        functools.partial(_decode_kernel, g_pages=g_pages),
        out_shape=[jax.ShapeDtypeStruct((nb, gw), F32), jax.ShapeDtypeStruct((nb, gw), F32)],
        grid_spec=pltpu.PrefetchScalarGridSpec(
            num_scalar_prefetch=1,
            grid=(nb,),
            in_specs=in_specs,
            out_specs=[out_full, out_full],
            scratch_shapes=[pltpu.VMEM((2, g_pages, page, gw), F32),
                            pltpu.VMEM((2, g_pages, page, gw), F32),
                            pltpu.VMEM((2, g_pages, page, MLA_KV_RANK), F32),
                            pltpu.VMEM((2, g_pages, page, MLA_ROPE), F32),
                            pltpu.SemaphoreType.DMA((4, 2)),
                            pltpu.VMEM((N_HEADS, nb, MLA_KV_RANK), F32),
                            pltpu.VMEM((N_HEADS, nb, MLA_KV_RANK), F32)]),
        compiler_params=pltpu.CompilerParams(
            dimension_semantics=("arbitrary",), vmem_limit_bytes=VMEM_LIMIT_BYTES),
        name="decode_attn",
    )(page_table, bias, *vm, k_pool, v_pool, ckv_pool, kr_pool)


def _swap_halves(w):
    half = w.shape[-1] // 2
    return jnp.concatenate([w[..., half:], w[..., :half]], axis=-1)


def _layer_weights(p, l):
    gw = GROUP_WIDTH
    w_in = p['w_in'][l]
    sizes = (gw, gw, gw, N_HEADS, gw, gw, gw, gw, gw, GLA_LOWRANK, MLA_Q_RANK, MLA_KV_RANK, MLA_ROPE)
    offs = np.concatenate([[0], np.cumsum(sizes)])
    fq, fk, fv, ff, su, gq, gk, gv, gg, glr, mq, mkv, mkr = [
        w_in[:, int(offs[i]):int(offs[i + 1])] for i in range(len(sizes))]
    d = w_in.shape[0]
    small = jnp.concatenate([mkr, _swap_halves(mkr), glr, ff,
                             jnp.zeros((d, LANES - 2 * MLA_ROPE - GLA_LOWRANK - N_HEADS), F32)], axis=1)
    wcat = jnp.concatenate([fq, fk, fv, su, gq, gk, gv, gg, mkv, small, mq,
                            jnp.zeros((d, N_ZCOLS - C_MQ - MLA_Q_RANK), F32)], axis=1).astype(BF16)

    w_uq = p['w_mla_uq'][l].reshape(MLA_Q_RANK, N_HEADS, HEAD_DIM + MLA_ROPE)
    nope, ropew = w_uq[..., :HEAD_DIM], w_uq[..., HEAD_DIM:]
    zpad = jnp.zeros((MLA_Q_RANK, N_HEADS, MLA_SLOT - HEAD_DIM - MLA_ROPE), F32)
    w_uq_a = jnp.concatenate([nope, ropew, zpad], axis=-1).reshape(MLA_Q_RANK, N_HEADS * MLA_SLOT)
    w_uq_b = jnp.concatenate([jnp.zeros_like(nope), _swap_halves(ropew), zpad],
                             axis=-1).reshape(MLA_Q_RANK, N_HEADS * MLA_SLOT)
    w_uk = p['w_mla_uk'][l]
    w_uk_slots = jnp.concatenate(
        [w_uk, jnp.zeros((MLA_KV_RANK, N_HEADS, MLA_SLOT - HEAD_DIM), F32)],
        axis=-1).reshape(MLA_KV_RANK, N_HEADS * MLA_SLOT)
    p_kr = jnp.concatenate([jnp.zeros((MLA_ROPE, HEAD_DIM), F32), jnp.eye(MLA_ROPE, dtype=F32),
                            jnp.zeros((MLA_ROPE, MLA_SLOT - HEAD_DIM - MLA_ROPE), F32)], axis=1)
    p_kr = jnp.tile(p_kr, (1, N_HEADS))

    abr, abi, bin_re, bin_im = _s5_discretize(p['s5_a_re'][l], p['s5_a_im'][l], p['s5_log_dt'][l],
                                              p['s5_b_re'][l], p['s5_b_im'][l])
    row = lambda a: a.reshape(1, -1).astype(F32)
    return dict(
        wcat=wcat, norm_mix_pre=row(p['norm_mix_pre'][l]), b_fox_f=row(p['b_fox_f'][l]),
        w_gla_gk=p['w_gla_gk'][l].astype(BF16), b_gla_gk=row(p['b_gla_gk'][l]),
        mla_q_norm=row(p['mla_q_norm'][l]), w_uq_a=w_uq_a.astype(BF16), w_uq_b=w_uq_b.astype(BF16),
        mla_kv_norm=row(p['mla_kv_norm'][l]), w_uk_slots=w_uk_slots.astype(BF16),
        w_uv=p['w_mla_uv'][l].reshape(MLA_KV_RANK, gw).astype(BF16), p_kr=p_kr.astype(BF16),
        w_uk_h=jnp.transpose(w_uk, (1, 0, 2)).astype(BF16),
        w_uv_h=jnp.transpose(p['w_mla_uv'][l], (1, 0, 2)).astype(BF16),
        s5_abar_re=abr, s5_abar_im=abi, s5_bin_re=bin_re, s5_bin_im=bin_im,
        s5_cout_re=_s5_out_blockdiag(p['s5_c_re'][l]), s5_cout_im=_s5_out_blockdiag(p['s5_c_im'][l]),
        s5_d=row(p['s5_d'][l]), w_s5_glu=p['w_s5_glu'][l].astype(BF16), b_s5_glu=row(p['b_s5_glu'][l]),
        gla_norm=row(p['gla_norm'][l]),
        w_out=p['w_out'][l].astype(BF16), norm_mix_post=row(p['norm_mix_post'][l]),
        norm_ffn_pre=row(p['norm_ffn_pre'][l]), norm_ffn_post=row(p['norm_ffn_post'][l]),
        w_ffn_gate=p['w_ffn_gate'][l].astype(BF16), w_ffn_up=p['w_ffn_up'][l].astype(BF16),
        w_ffn_down=p['w_ffn_down'][l].astype(BF16))


def _rope_tables(pos):
    half = MLA_ROPE // 2
    inv = ROPE_THETA ** (-jnp.arange(half, dtype=F32) * 2.0 / MLA_ROPE)
    ang = pos.astype(F32)[:, None] * inv[None, :]
    cos, sin = jnp.cos(ang), jnp.sin(ang)
    n = pos.shape[0]
    c32 = jnp.concatenate([cos, cos], axis=-1)
    s32 = jnp.concatenate([-sin, sin], axis=-1)
    ones = jnp.ones((n, HEAD_DIM), F32)
    zeros = jnp.zeros((n, HEAD_DIM), F32)
    zpad = jnp.zeros((n, MLA_SLOT - HEAD_DIM - MLA_ROPE), F32)
    tqc = jnp.tile(jnp.concatenate([ones, c32, zpad], axis=-1), (1, N_HEADS))
    tqs = jnp.tile(jnp.concatenate([zeros, s32, zpad], axis=-1), (1, N_HEADS))
    return tqc, tqs, jnp.concatenate([c32, s32], axis=-1)


def kernel(x_prompt, x_sample, cache_fox_k, cache_fox_v, cache_fox_logf, cache_mla_ckv, cache_mla_krope, state_s5_re, state_s5_im, state_gla, page_table, meta_tokens, norm_mix_pre, norm_mix_post, norm_ffn_pre, norm_ffn_post, w_in, b_fox_f, s5_a_re, s5_a_im, s5_log_dt, s5_b_re, s5_b_im, s5_c_re, s5_c_im, s5_d, w_s5_glu, b_s5_glu, w_gla_gk, b_gla_gk, gla_norm, mla_q_norm, w_mla_uq, mla_kv_norm, w_mla_uk, w_mla_uv, w_out, w_ffn_gate, w_ffn_up, w_ffn_down):
    params = dict(
        norm_mix_pre=norm_mix_pre, norm_mix_post=norm_mix_post, norm_ffn_pre=norm_ffn_pre,
        norm_ffn_post=norm_ffn_post, w_in=w_in, b_fox_f=b_fox_f, s5_a_re=s5_a_re, s5_a_im=s5_a_im,
        s5_log_dt=s5_log_dt, s5_b_re=s5_b_re, s5_b_im=s5_b_im, s5_c_re=s5_c_re, s5_c_im=s5_c_im,
        s5_d=s5_d, w_s5_glu=w_s5_glu, b_s5_glu=b_s5_glu, w_gla_gk=w_gla_gk, b_gla_gk=b_gla_gk,
        gla_norm=gla_norm, mla_q_norm=mla_q_norm, w_mla_uq=w_mla_uq, mla_kv_norm=mla_kv_norm,
        w_mla_uk=w_mla_uk, w_mla_uv=w_mla_uv, w_out=w_out, w_ffn_gate=w_ffn_gate,
        w_ffn_up=w_ffn_up, w_ffn_down=w_ffn_down)
    depth = w_in.shape[0]
    nbp, seq, d = x_prompt.shape
    nbs, dec_seq, _ = x_sample.shape
    assert dec_seq == 1, "the sample path handles one new token per sequence"
    t = seq + N_META
    n_pages, page = page_table.shape[1], cache_fox_k.shape[2]
    past_len = n_pages * page
    gw = GROUP_WIDTH

    hp = jnp.concatenate([jnp.broadcast_to(meta_tokens.astype(x_prompt.dtype)[None], (nbp, N_META, d)),
                          x_prompt], axis=1)
    hs = x_sample.reshape(1, nbs, d)
    tabs_p = _rope_tables(jnp.arange(t))
    tabs_s = _rope_tables(jnp.full((nbs,), past_len, jnp.int32))

    rows_p, rows_s = [], []
    for l in range(depth):
        lw = _layer_weights(params, l)

        (fq, fk, fv, fkb, fvb, logf, c, su_tm, gla_in, qcat, kcat, vmla, ckv, kr) = _proj_call(
            hp, lw, tabs_p, cumsum=True)
        fox_o = _attn_call(fq, fkb, fvb, c)
        mla_o = _attn_call(qcat, kcat, vmla)
        zeros_state = jnp.zeros((nbp, S5_LANES), F32)
        s5_o, s5_re, s5_im = _s5_call(su_tm.reshape(t * nbp, gw), zeros_state, zeros_state, lw, nbp)
        gla_o, gla_st = _gla_call(gla_in, jnp.zeros((nbp, gw, HEAD_DIM), F32), lw['gla_norm'])
        hp = _finish_call(hp, fox_o, s5_o.reshape(t, nbp * gw), gla_o, mla_o, lw)
        rows_p.append((fk.reshape(nbp, t, N_HEADS, HEAD_DIM), fv.reshape(nbp, t, N_HEADS, HEAD_DIM),
                       logf, ckv, kr, s5_re.reshape(nbp, S5_GROUPS, S5_STATE),
                       s5_im.reshape(nbp, S5_GROUPS, S5_STATE),
                       jnp.swapaxes(gla_st.reshape(nbp, N_HEADS, HEAD_DIM, HEAD_DIM), 2, 3)))

        (fq, fk, fv, fkb, fvb, logf, c, su_tm, gla_in, qcat, kcat, vmla, ckv, kr) = _proj_call(
            hs, lw, tabs_s, cumsum=False)
        n_pool = cache_fox_k.shape[1]
        logf_t = jnp.pad(jnp.swapaxes(cache_fox_logf[l].astype(F32), 1, 2),
                         ((0, 0), (0, SUBLANES - N_HEADS), (0, 0)))
        bias = _fox_bias_call(page_table, logf_t)
        cnew8 = jnp.concatenate([logf[0], jnp.zeros((nbs, SUBLANES - N_HEADS), F32)], axis=1)
        fox_o, mla_o = _decode_call(
            page_table, bias, fq[0].astype(F32), qcat[0].astype(F32), cnew8, fk[0], fv[0], ckv[0], kr[0],
            lw['w_uk_h'], lw['w_uv_h'],
            cache_fox_k[l].reshape(n_pool, page, gw), cache_fox_v[l].reshape(n_pool, page, gw),
            cache_mla_ckv[l], cache_mla_krope[l])
        s5_o, s5_re, s5_im = _s5_call(su_tm, state_s5_re[l].reshape(nbs, S5_LANES),
                                      state_s5_im[l].reshape(nbs, S5_LANES), lw, nbs)
        gla_pad = jnp.concatenate([jnp.zeros((nbs, SUBLANES - 1, 5 * gw), F32),
                                   gla_in.reshape(nbs, 1, 5 * gw)], axis=1)
        s0t = jnp.swapaxes(state_gla[l].astype(F32), 2, 3).reshape(nbs, gw, HEAD_DIM)
        gla_o, gla_st = _gla_call(gla_pad, s0t, lw['gla_norm'])
        gla_o = gla_o[:, SUBLANES - 1, :].reshape(1, nbs, gw)
        hs = _finish_call(hs, fox_o.reshape(1, nbs, gw), s5_o, gla_o, mla_o.reshape(1, nbs, gw), lw)
        rows_s.append((fk.reshape(nbs, 1, N_HEADS, HEAD_DIM), fv.reshape(nbs, 1, N_HEADS, HEAD_DIM),
                       logf.reshape(nbs, 1, N_HEADS), ckv.reshape(nbs, 1, MLA_KV_RANK),
                       kr.reshape(nbs, 1, MLA_ROPE), s5_re.reshape(nbs, S5_GROUPS, S5_STATE),
                       s5_im.reshape(nbs, S5_GROUPS, S5_STATE),
                       jnp.swapaxes(gla_st.reshape(nbs, N_HEADS, HEAD_DIM, HEAD_DIM), 2, 3)))

    fk_p, fv_p, flf_p, ckv_p, kr_p, s5re_p, s5im_p, gla_p = [jnp.stack(r) for r in zip(*rows_p)]
    fk_s, fv_s, flf_s, ckv_s, kr_s, s5re_s, s5im_s, gla_s = [jnp.stack(r) for r in zip(*rows_s)]
    y_prompt = hp[:, N_META:]
    y_sample = hs.reshape(nbs, 1, d)
    return (y_prompt, y_sample, fk_p, fk_s, fv_p, fv_s, flf_p, flf_s, ckv_p, ckv_s, kr_p, kr_s,
            s5re_p, s5re_s, s5im_p, s5im_s, gla_p, gla_s)
```

```python
import functools
import math

import jax
import jax.numpy as jnp
import numpy as np
from jax import lax
from jax.experimental import pallas as pl
from jax.experimental.pallas import tpu as pltpu

F32 = jnp.float32
BF16 = jnp.bfloat16
HIGHEST = lax.Precision.HIGHEST

N_META = 16
HEAD_DIM = 64
NORM_EPS = 1e-6
N_HEADS = 4
GROUP_WIDTH = N_HEADS * HEAD_DIM
S5_GC = 16
S5_GROUPS = GROUP_WIDTH // S5_GC
S5_STATE = 64
S5_LANES = S5_GROUPS * S5_STATE
GLA_LOWRANK = 16
GLA_GATE_NORM = 16.0
MLA_ROPE = HEAD_DIM // 2
MLA_Q_RANK = 3 * GROUP_WIDTH // 4
MLA_KV_RANK = GROUP_WIDTH // 2
FOX_SCALE = HEAD_DIM ** -0.5
GLA_SCALE = HEAD_DIM ** -0.5
MLA_SCALE = (HEAD_DIM + MLA_ROPE) ** -0.5
ROPE_THETA = 10000.0
MLA_SLOT = 128

SUBLANES = 8
LANES = 128
VMEM_LIMIT_BYTES = 56 * 1024 * 1024

PROJ_TILE = 688
ATTN_TILE = 384
S5_CHUNK = 48
GLA_CHUNK = 48
GLA_BATCH_BLOCK = 4
FFN_TILE = 344
DECODE_PAGES_PER_STEP = 16

C_FOX = 0
C_S5 = 768
C_GLA = 1024
C_MKV = 2048
C_SMALL = 2176
C_MQ = 2304
N_ZCOLS = 2560


def _pick_tile(n, target):
    best = None
    for t in range(SUBLANES, min(n, target) + 1, SUBLANES):
        if n % t == 0:
            best = t
    return best if best is not None else n


def _rms(x, w):
    return x * lax.rsqrt(jnp.mean(x * x, axis=-1, keepdims=True) + NORM_EPS) * w


def _log_sigmoid(x):
    return jnp.minimum(x, 0.0) - jnp.log1p(jnp.exp(-jnp.abs(x)))


def _sigmoid(x):
    return 1.0 / (1.0 + jnp.exp(-x))


def _dot(a, b):
    return jnp.dot(a, b, preferred_element_type=F32)


def _dot_nt(a, b):
    return lax.dot_general(a, b, (((1,), (1,)), ((), ())), preferred_element_type=F32)


def _dot_tn(a, b):
    return lax.dot_general(a, b, (((0,), (0,)), ((), ())), preferred_element_type=F32)


def _const_spec(shape):
    nd = len(shape)
    return pl.BlockSpec(shape, lambda *_: (0,) * nd)


def _proj_kernel(x_ref, nw_ref, w_ref, bff_ref, wgk_ref, bgk_ref, qnw_ref, wuqa_ref, wuqb_ref,
                 kvnw_ref, wuk_ref, wuv_ref, pkr_ref, tqc_ref, tqs_ref, tkr_ref,
                 fq_ref, fk_ref, fv_ref, fkb_ref, fvb_ref, logf_ref, c_ref, su_ref, gla_ref,
                 qcat_ref, kcat_ref, vmla_ref, ckv_ref, kr_ref, carry_ref, *, cumsum):
    tm = x_ref.shape[0]
    hn = _rms(x_ref[...], nw_ref[...]).astype(BF16)

    zf = _dot(hn, w_ref[:, C_FOX:C_FOX + 3 * GROUP_WIDTH])
    fq_ref[...] = (zf[:, :GROUP_WIDTH] * FOX_SCALE).astype(BF16)
    fk = zf[:, GROUP_WIDTH:2 * GROUP_WIDTH]
    fv = zf[:, 2 * GROUP_WIDTH:]
    fk_ref[...] = fk
    fv_ref[...] = fv
    fkb_ref[...] = fk.astype(BF16)
    fvb_ref[...] = fv.astype(BF16)

    su_ref[...] = _dot(hn, w_ref[:, C_S5:C_S5 + GROUP_WIDTH])

    zs = _dot(hn, w_ref[:, C_SMALL:C_SMALL + LANES])
    mkr = zs[:, 0:MLA_ROPE]
    mkr_sw = zs[:, MLA_ROPE:2 * MLA_ROPE]
    glr = zs[:, 2 * MLA_ROPE:2 * MLA_ROPE + GLA_LOWRANK]
    ff = zs[:, 2 * MLA_ROPE + GLA_LOWRANK:2 * MLA_ROPE + GLA_LOWRANK + N_HEADS]

    logf = _log_sigmoid(ff + bff_ref[...])
    logf_ref[...] = logf
    if cumsum:
        ti = pl.program_id(1)

        @pl.when(ti == 0)
        def _():
            carry_ref[...] = jnp.zeros_like(carry_ref)

        row = lax.broadcasted_iota(jnp.int32, (tm, tm), 0)
        col = lax.broadcasted_iota(jnp.int32, (tm, tm), 1)
        tri = (row >= col).astype(F32)
        c = jnp.dot(tri, logf, precision=HIGHEST, preferred_element_type=F32) + carry_ref[...]
        c_ref[...] = c
        carry_ref[...] = c[tm - 1:tm, :]
    else:
        c_ref[...] = logf

    tkr = tkr_ref[...]
    kr = mkr * tkr[:, :MLA_ROPE] + mkr_sw * tkr[:, MLA_ROPE:]
    kr_ref[...] = kr

    glog = _log_sigmoid(_dot(glr.astype(BF16), wgk_ref[...]) + bgk_ref[...]) * (1.0 / GLA_GATE_NORM)
    zg = _dot(hn, w_ref[:, C_GLA:C_GLA + 4 * GROUP_WIDTH])
    gla_ref[:, 0:GROUP_WIDTH] = zg[:, 0:GROUP_WIDTH] * GLA_SCALE
    gla_ref[:, GROUP_WIDTH:3 * GROUP_WIDTH] = zg[:, GROUP_WIDTH:3 * GROUP_WIDTH]
    gla_ref[:, 3 * GROUP_WIDTH:4 * GROUP_WIDTH] = glog
    gla_ref[:, 4 * GROUP_WIDTH:5 * GROUP_WIDTH] = zg[:, 3 * GROUP_WIDTH:]

    mkv = _dot(hn, w_ref[:, C_MKV:C_MKV + MLA_KV_RANK])
    ckv = _rms(mkv, kvnw_ref[...])
    ckv_ref[...] = ckv
    ckv_b = ckv.astype(BF16)
    kcat = _dot(ckv_b, wuk_ref[...]) + _dot(kr.astype(BF16), pkr_ref[...])
    kcat_ref[...] = kcat.astype(BF16)
    vmla_ref[...] = _dot(ckv_b, wuv_ref[...]).astype(BF16)

    mq = _dot(hn, w_ref[:, C_MQ:C_MQ + MLA_Q_RANK])
    qln = _rms(mq, qnw_ref[...]).astype(BF16)
    qcat = _dot(qln, wuqa_ref[...]) * tqc_ref[...] + _dot(qln, wuqb_ref[...]) * tqs_ref[...]
    qcat_ref[...] = (qcat * MLA_SCALE).astype(BF16)


def _proj_call(x3, lw, tabs, *, cumsum):
    nb, t, d = x3.shape
    tm = _pick_tile(t, PROJ_TILE)
    nt = t // tm
    tqc, tqs, tkr = tabs
    gw = GROUP_WIDTH
    qs = N_HEADS * MLA_SLOT

    def tok(width):
        return pl.BlockSpec((None, tm, width), lambda b, i: (b, i, 0))

    def tab(width):
        return pl.BlockSpec((tm, width), lambda b, i: (i, 0))

    consts = [lw['norm_mix_pre'], lw['wcat'], lw['b_fox_f'], lw['w_gla_gk'], lw['b_gla_gk'],
              lw['mla_q_norm'], lw['w_uq_a'], lw['w_uq_b'], lw['mla_kv_norm'], lw['w_uk_slots'],
              lw['w_uv'], lw['p_kr']]
    in_specs = [tok(d)] + [_const_spec(c.shape) for c in consts] + [tab(qs), tab(qs), tab(2 * MLA_ROPE)]
    out_shape = [
        jax.ShapeDtypeStruct((nb, t, gw), BF16),
        jax.ShapeDtypeStruct((nb, t, gw), F32),
        jax.ShapeDtypeStruct((nb, t, gw), F32),
        jax.ShapeDtypeStruct((nb, t, gw), BF16),
        jax.ShapeDtypeStruct((nb, t, gw), BF16),
        jax.ShapeDtypeStruct((nb, t, N_HEADS), F32),
        jax.ShapeDtypeStruct((nb, t, N_HEADS), F32),
        jax.ShapeDtypeStruct((t, nb * gw), F32),
        jax.ShapeDtypeStruct((nb, t, 5 * gw), F32),
        jax.ShapeDtypeStruct((nb, t, qs), BF16),
        jax.ShapeDtypeStruct((nb, t, qs), BF16),
        jax.ShapeDtypeStruct((nb, t, gw), BF16),
        jax.ShapeDtypeStruct((nb, t, MLA_KV_RANK), F32),
        jax.ShapeDtypeStruct((nb, t, MLA_ROPE), F32),
    ]
    out_specs = [tok(gw), tok(gw), tok(gw), tok(gw), tok(gw), tok(N_HEADS), tok(N_HEADS),
                 pl.BlockSpec((tm, gw), lambda b, i: (i, b)),
                 tok(5 * gw), tok(qs), tok(qs), tok(gw), tok(MLA_KV_RANK), tok(MLA_ROPE)]
    return pl.pallas_call(
        functools.partial(_proj_kernel, cumsum=cumsum),
        out_shape=out_shape,
        grid=(nb, nt),
        in_specs=in_specs,
        out_specs=out_specs,
        scratch_shapes=[pltpu.VMEM((1, N_HEADS), F32)],
        compiler_params=pltpu.CompilerParams(
            dimension_semantics=("arbitrary", "arbitrary"), vmem_limit_bytes=VMEM_LIMIT_BYTES),
        name="proj",
    )(x3, *consts, tqc, tqs, tkr)


def _attn_kernel(*refs, dk, has_bias):
    if has_bias:
        q_ref, k_ref, v_ref, c_ref, o_ref, qp_ref, kp_ref, vp_ref, cp_ref = refs
    else:
        q_ref, k_ref, v_ref, o_ref, qp_ref, kp_ref, vp_ref = refs
    t = k_ref.shape[0]
    tq = o_ref.shape[0]
    tp = kp_ref.shape[0]
    qi = pl.program_id(1)

    @pl.when(qi == 0)
    def _():
        pairs = [(q_ref, qp_ref), (k_ref, kp_ref), (v_ref, vp_ref)]
        if has_bias:
            pairs.append((c_ref, cp_ref))
        for src, dst in pairs:
            dst[0:t, :] = src[...]
            dst[t:tp, :] = jnp.zeros((tp - t, dst.shape[1]), dst.dtype)

    q0 = pl.multiple_of(qi * tq, tq)
    causal = (lax.broadcasted_iota(jnp.int32, (tq, tq), 0)
              <= lax.broadcasted_iota(jnp.int32, (tq, tq), 1))

    qs = [qp_ref[pl.ds(q0, tq), h * dk:(h + 1) * dk] for h in range(N_HEADS)]

    def step(carry, k0, masked):
        new = []
        for h in range(N_HEADS):
            m, l, acc = carry[h]
            s = _dot_nt(kp_ref[pl.ds(k0, tq), h * dk:(h + 1) * dk], qs[h])
            if has_bias:
                s = s - cp_ref[pl.ds(k0, tq), h:h + 1]
            if masked:
                s = jnp.where(causal, s, -jnp.inf)
            m_new = jnp.maximum(m, jnp.max(s, axis=0, keepdims=True))
            alpha = jnp.exp(m - m_new)
            p = jnp.exp(s - m_new)
            l = alpha * l + jnp.sum(p, axis=0, keepdims=True)
            v = vp_ref[pl.ds(k0, tq), h * HEAD_DIM:(h + 1) * HEAD_DIM]
            acc = alpha * acc + _dot_tn(v, p.astype(BF16))
            new.append((m_new, l, acc))
        return tuple(new)

    def body(kj, carry):
        return step(carry, pl.multiple_of(kj * tq, tq), False)

    init = tuple((jnp.full((1, tq), -jnp.inf, F32), jnp.zeros((1, tq), F32),
                  jnp.zeros((HEAD_DIM, tq), F32)) for _ in range(N_HEADS))
    carry = step(lax.fori_loop(0, qi, body, init), q0, True)
    outs = [acc / l for (_, l, acc) in carry]
    for pair in range(N_HEADS // 2):
        o_ref[:, pair * 2 * HEAD_DIM:(pair + 1) * 2 * HEAD_DIM] = jnp.concatenate(
            outs[2 * pair:2 * pair + 2], axis=0).T


def _attn_call(q, k, v, c=None):
    nb, t, qw = q.shape
    dk = qw // N_HEADS
    tq = ATTN_TILE
    nq = pl.cdiv(t, tq)
    tp = nq * tq
    has_bias = c is not None

    def seq(width):
        return pl.BlockSpec((None, t, width), lambda b, i: (b, 0, 0))

    in_specs = [seq(qw), seq(qw), seq(GROUP_WIDTH)]
    args = [q, k, v]
    scratch = [pltpu.VMEM((tp, qw), BF16), pltpu.VMEM((tp, qw), BF16), pltpu.VMEM((tp, GROUP_WIDTH), BF16)]
    if has_bias:
        in_specs.append(seq(N_HEADS))
        args.append(c)
        scratch.append(pltpu.VMEM((tp, N_HEADS), F32))
    return pl.pallas_call(
        functools.partial(_attn_kernel, dk=dk, has_bias=has_bias),
        out_shape=jax.ShapeDtypeStruct((nb, tp, GROUP_WIDTH), F32),
        grid=(nb, nq),
        in_specs=in_specs,
        out_specs=pl.BlockSpec((None, tq, GROUP_WIDTH), lambda b, i: (b, i, 0)),
        scratch_shapes=scratch,
        compiler_params=pltpu.CompilerParams(
            dimension_semantics=("arbitrary", "arbitrary"), vmem_limit_bytes=VMEM_LIMIT_BYTES),
        name="fox_attn" if has_bias else "mla_attn",
    )(*args)


def _s5_disc_kernel(are_ref, aim_ref, ldt_ref, bre_ref, bim_ref,
                    abr_ref, abi_ref, bbr_ref, bbi_ref):
    dt = jnp.exp(ldt_ref[...])
    ar = are_ref[...]
    ai = aim_ref[...]
    mag = jnp.exp(ar * dt)
    abr = mag * jnp.cos(ai * dt)
    abi = mag * jnp.sin(ai * dt)
    nr = abr - 1.0
    ni = abi
    den = ar * ar + ai * ai
    fr = (nr * ar + ni * ai) / den
    fi = (ni * ar - nr * ai) / den
    abr_ref[...] = abr
    abi_ref[...] = abi
    for c in range(S5_GC):
        br = bre_ref[c]
        bi = bim_ref[c]
        bbr_ref[c] = fr * br - fi * bi
        bbi_ref[c] = fr * bi + fi * br


def _s5_discretize(a_re, a_im, log_dt, b_re, b_im):
    g, p = a_re.shape
    bt = lambda b: jnp.transpose(b, (2, 0, 1))
    outs = pl.pallas_call(
        _s5_disc_kernel,
        out_shape=[jax.ShapeDtypeStruct((g, p), F32), jax.ShapeDtypeStruct((g, p), F32),
                   jax.ShapeDtypeStruct((S5_GC, g, p), F32), jax.ShapeDtypeStruct((S5_GC, g, p), F32)],
        name="s5_discretize",
    )(a_re, a_im, log_dt.reshape(g, 1), bt(b_re), bt(b_im))
    abr, abi, bbr, bbi = outs
    eye = jnp.eye(g, dtype=F32)

    def in_blockdiag(bb):
        m = jnp.transpose(bb, (1, 0, 2))[:, :, None, :] * eye[:, None, :, None]
        return m.reshape(g * S5_GC, g * p).astype(BF16)

    return abr.reshape(1, g * p), abi.reshape(1, g * p), in_blockdiag(bbr), in_blockdiag(bbi)


def _s5_out_blockdiag(c):
    g = c.shape[0]
    eye = jnp.eye(g, dtype=F32)
    m = jnp.transpose(c, (0, 2, 1))[:, :, None, :] * eye[:, None, :, None]
    return m.reshape(g * c.shape[2], g * c.shape[1]).astype(BF16)


def _gelu_tanh(x):
    return 0.5 * x * (1.0 + jnp.tanh(math.sqrt(2.0 / math.pi) * (x + 0.044715 * (x * x * x))))


def _s5_kernel(u_ref, h0r_ref, h0i_ref, ar_ref, ai_ref, bre_ref, bim_ref, cre_ref, cim_ref,
               d_ref, wglu_ref, bglu_ref, o_ref, hr_ref, hi_ref, sr_ref, si_ref, *, nb):
    i = pl.program_id(0)
    tc = u_ref.shape[0] // nb

    @pl.when(i == 0)
    def _():
        hr_ref[...] = h0r_ref[...]
        hi_ref[...] = h0i_ref[...]

    u = u_ref[...]
    ub = u.astype(BF16)
    sr_ref[...] = _dot(ub, bre_ref[...])
    si_ref[...] = _dot(ub, bim_ref[...])
    ar = ar_ref[...]
    ai = ai_ref[...]

    def step(t, carry):
        hr, hi = carry
        r0 = pl.multiple_of(t * nb, nb)
        nhr = ar * hr - ai * hi + sr_ref[pl.ds(r0, nb), :]
        nhi = ar * hi + ai * hr + si_ref[pl.ds(r0, nb), :]
        sr_ref[pl.ds(r0, nb), :] = nhr
        si_ref[pl.ds(r0, nb), :] = nhi
        return nhr, nhi

    hr, hi = lax.fori_loop(0, tc, step, (hr_ref[...], hi_ref[...]))
    hr_ref[...] = hr
    hi_ref[...] = hi

    y = (_dot(sr_ref[...].astype(BF16), cre_ref[...]) - _dot(si_ref[...].astype(BF16), cim_ref[...])
         + d_ref[...] * u)
    g = _gelu_tanh(y)
    o_ref[...] = g * _sigmoid(_dot(g.astype(BF16), wglu_ref[...]) + bglu_ref[...])


def _s5_call(u_tm, h0r, h0i, lw, nb):
    rows = u_tm.shape[0]
    t = rows // nb
    tc = _pick_tile(t, S5_CHUNK) if t >= SUBLANES else t
    r = tc * nb
    consts = [lw['s5_abar_re'], lw['s5_abar_im'], lw['s5_bin_re'], lw['s5_bin_im'],
              lw['s5_cout_re'], lw['s5_cout_im'], lw['s5_d'], lw['w_s5_glu'], lw['b_s5_glu']]
    state_spec = _const_spec((nb, S5_LANES))
    return pl.pallas_call(
        functools.partial(_s5_kernel, nb=nb),
        out_shape=[jax.ShapeDtypeStruct((rows, GROUP_WIDTH), F32),
                   jax.ShapeDtypeStruct((nb, S5_LANES), F32),
                   jax.ShapeDtypeStruct((nb, S5_LANES), F32)],
        grid=(t // tc,),
        in_specs=[pl.BlockSpec((r, GROUP_WIDTH), lambda i: (i, 0)), state_spec, state_spec]
                 + [_const_spec(c.shape) for c in consts],
        out_specs=[pl.BlockSpec((r, GROUP_WIDTH), lambda i: (i, 0)), state_spec, state_spec],
        scratch_shapes=[pltpu.VMEM((r, S5_LANES), F32), pltpu.VMEM((r, S5_LANES), F32)],
        compiler_params=pltpu.CompilerParams(
            dimension_semantics=("arbitrary",), vmem_limit_bytes=VMEM_LIMIT_BYTES),
        name="s5_scan",
    )(u_tm, h0r, h0i, *consts)


def _gla_kernel(x_ref, s0_ref, nw_ref, o_ref, st_ref):
    ci = pl.program_id(1)
    bb, c = x_ref.shape[0], x_ref.shape[1]
    gw = GROUP_WIDTH

    @pl.when(ci == 0)
    def _():
        st_ref[...] = s0_ref[...]

    row = lax.broadcasted_iota(jnp.int32, (c, c), 0)
    col = lax.broadcasted_iota(jnp.int32, (c, c), 1)
    causal = row >= col
    tri = causal.astype(F32)

    for bi in range(bb):
        q = x_ref[bi, :, 0:gw]
        k = x_ref[bi, :, gw:2 * gw]
        v = x_ref[bi, :, 2 * gw:3 * gw]
        g = x_ref[bi, :, 3 * gw:4 * gw]
        gate = x_ref[bi, :, 4 * gw:5 * gw]
        bc = jnp.dot(tri, g, precision=HIGHEST, preferred_element_type=F32)
        b_last = bc[c - 1:c, :]
        b_mid = bc[c // 2:c // 2 + 1, :]
        q_in = (q * jnp.exp(bc - b_mid)).astype(BF16)
        k_in = (k * jnp.exp(b_mid - bc)).astype(BF16)
        q_st = (q * jnp.exp(bc)).astype(BF16)
        k_st = (k * jnp.exp(b_last - bc)).astype(BF16)
        vb = v.astype(BF16)
        decay = jnp.exp(b_last)
        for h in range(N_HEADS):
            sl = slice(h * HEAD_DIM, (h + 1) * HEAD_DIM)
            att = jnp.where(causal, _dot_nt(q_in[:, sl], k_in[:, sl]), 0.0)
            st = st_ref[bi, sl, :]
            o = _dot_nt(q_st[:, sl], st.astype(BF16)) + _dot(att.astype(BF16), vb[:, sl])
            st_ref[bi, sl, :] = st * decay[:, sl] + _dot_tn(vb[:, sl], k_st[:, sl])
            o = _rms(o, nw_ref[...])
            gt = gate[:, sl]
            o_ref[bi, :, sl] = o * (gt * _sigmoid(gt))


def _gla_call(gla_in, s0t, norm_w):
    nb, t, w = gla_in.shape
    c = _pick_tile(t, GLA_CHUNK)
    bb = math.gcd(nb, GLA_BATCH_BLOCK)
    return pl.pallas_call(
        _gla_kernel,
        out_shape=[jax.ShapeDtypeStruct((nb, t, GROUP_WIDTH), F32),
                   jax.ShapeDtypeStruct((nb, GROUP_WIDTH, HEAD_DIM), F32)],
        grid=(nb // bb, t // c),
        in_specs=[pl.BlockSpec((bb, c, w), lambda b, i: (b, i, 0)),
                  pl.BlockSpec((bb, GROUP_WIDTH, HEAD_DIM), lambda b, i: (b, 0, 0)),
                  _const_spec(norm_w.shape)],
        out_specs=[pl.BlockSpec((bb, c, GROUP_WIDTH), lambda b, i: (b, i, 0)),
                   pl.BlockSpec((bb, GROUP_WIDTH, HEAD_DIM), lambda b, i: (b, 0, 0))],
        compiler_params=pltpu.CompilerParams(
            dimension_semantics=("arbitrary", "arbitrary"), vmem_limit_bytes=VMEM_LIMIT_BYTES),
        name="gla",
    )(gla_in, s0t, norm_w)


def _finish_kernel(x_ref, fox_ref, s5_ref, gla_ref, mla_ref, wout_ref, npost_ref, npre_ref,
                   nffn_ref, wg_ref, wu_ref, wd_ref, o_ref):
    gw = GROUP_WIDTH
    mix = (_dot(fox_ref[...].astype(BF16), wout_ref[0:gw, :])
           + _dot(s5_ref[...].astype(BF16), wout_ref[gw:2 * gw, :])
           + _dot(gla_ref[...].astype(BF16), wout_ref[2 * gw:3 * gw, :])
           + _dot(mla_ref[...].astype(BF16), wout_ref[3 * gw:4 * gw, :]))
    x1 = x_ref[...] + _rms(mix, npost_ref[...])
    hf = _rms(x1, npre_ref[...]).astype(BF16)
    gate = _dot(hf, wg_ref[...])
    act = (gate * _sigmoid(gate) * _dot(hf, wu_ref[...])).astype(BF16)
    f = _dot(act, wd_ref[...])
    o_ref[...] = x1 + _rms(f, nffn_ref[...])


def _finish_call(x3, fox_o, s5_tm, gla_o, mla_o, lw):
    nb, t, d = x3.shape
    tm = _pick_tile(t, FFN_TILE)
    gw = GROUP_WIDTH

    def tok(width):
        return pl.BlockSpec((None, tm, width), lambda b, i: (b, i, 0))

    consts = [lw['w_out'], lw['norm_mix_post'], lw['norm_ffn_pre'], lw['norm_ffn_post'],
              lw['w_ffn_gate'], lw['w_ffn_up'], lw['w_ffn_down']]
    const_specs = [pl.BlockSpec(c.shape, lambda b, i: (0, 0), pipeline_mode=pl.Buffered(1))
                   for c in consts]
    return pl.pallas_call(
        _finish_kernel,
        out_shape=jax.ShapeDtypeStruct((nb, t, d), F32),
        grid=(nb, t // tm),
        in_specs=[tok(d), tok(gw), pl.BlockSpec((tm, gw), lambda b, i: (i, b)), tok(gw), tok(gw)]
                 + const_specs,
        out_specs=tok(d),
        compiler_params=pltpu.CompilerParams(
            dimension_semantics=("arbitrary", "arbitrary"), vmem_limit_bytes=VMEM_LIMIT_BYTES),
        name="mix_ffn",
    )(x3, fox_o, s5_tm, gla_o, mla_o, *consts)


def _fox_bias_kernel(pt_ref, logf_hbm, o_ref, buf_ref, sem_ref):
    b = pl.program_id(0)
    nb = pl.num_programs(0)
    n_pages = buf_ref.shape[1]
    page = buf_ref.shape[3]
    rows = n_pages * SUBLANES

    def page_copy(bb, slot, p):
        return pltpu.make_async_copy(logf_hbm.at[pt_ref[bb, p]], buf_ref.at[slot, p], sem_ref.at[slot])

    def start_all(bb, slot):
        def go(p, _):
            page_copy(bb, slot, p).start()
            return 0
        lax.fori_loop(0, n_pages, go, 0)

    @pl.when(b == 0)
    def _():
        start_all(0, 0)

    slot = b % 2

    @pl.when(b + 1 < nb)
    def _():
        start_all(b + 1, 1 - slot)

    def wait_one(p, _):
        page_copy(b, slot, p).wait()
        return 0
    lax.fori_loop(0, n_pages, wait_one, 0)

    x = buf_ref[slot].reshape(rows, page)
    hi = x.astype(BF16)
    r1 = x - hi.astype(F32)
    mid = r1.astype(BF16)
    lo = (r1 - mid.astype(F32)).astype(BF16)
    j = lax.broadcasted_iota(jnp.int32, (page, 2 * page), 0)
    s = lax.broadcasted_iota(jnp.int32, (page, 2 * page), 1)
    w = jnp.where((j > s) | (s >= page), 1.0, 0.0).astype(BF16)
    acc = (_dot(hi, w) + _dot(mid, w) + _dot(lo, w)).reshape(n_pages, SUBLANES, 2 * page)
    run = jnp.zeros((SUBLANES, page), F32)
    for p in reversed(range(n_pages)):
        o_ref[p] = acc[p, :, :page] + run
        run = run + acc[p, :, page:]


def _fox_bias_call(page_table, logf_t):
    nb, n_pages = page_table.shape
    page = logf_t.shape[2]
    rows = n_pages * SUBLANES
    return pl.pallas_call(
        _fox_bias_kernel,
        out_shape=jax.ShapeDtypeStruct((nb, n_pages, SUBLANES, page), F32),
        grid_spec=pltpu.PrefetchScalarGridSpec(
            num_scalar_prefetch=1,
            grid=(nb,),
            in_specs=[pl.BlockSpec(memory_space=pl.ANY)],
            out_specs=pl.BlockSpec((None, n_pages, SUBLANES, page), lambda b, pt: (b, 0, 0, 0)),
            scratch_shapes=[pltpu.VMEM((2, n_pages, SUBLANES, page), F32),
                            pltpu.SemaphoreType.DMA((2,))]),
        compiler_params=pltpu.CompilerParams(
            dimension_semantics=("arbitrary",), vmem_limit_bytes=VMEM_LIMIT_BYTES),
        name="fox_decode_bias",
    )(page_table, logf_t)


def _decode_kernel(pt_ref, bias_ref, fq_ref, fqt_ref, qcat_ref, cnew_ref, knew_ref, vnew_ref, ckvnew_ref,
                   krnew_ref, wuk_ref, wuv_ref, k_hbm, v_hbm, ckv_hbm, kr_hbm,
                   fox_ref, mla_ref,
                   kbuf, vbuf, cbuf, rbuf, sem_ref, qabs_ref, olat_ref, *, g_pages, layer):
    b = pl.program_id(0)
    nb = pl.num_programs(0)
    n_pages = bias_ref.shape[0]
    nc = n_pages // g_pages
    page = bias_ref.shape[2]
    gw = GROUP_WIDTH
    nb_static = fq_ref.shape[0]

    def page_copies(bb, ci, slot, j):
        pid = pt_ref[bb, ci * g_pages + j]
        return [pltpu.make_async_copy(k_hbm.at[layer, pid], kbuf.at[slot, j], sem_ref.at[0, slot]),
                pltpu.make_async_copy(v_hbm.at[layer, pid], vbuf.at[slot, j], sem_ref.at[1, slot]),
                pltpu.make_async_copy(ckv_hbm.at[layer, pid], cbuf.at[slot, j], sem_ref.at[2, slot]),
                pltpu.make_async_copy(kr_hbm.at[layer, pid], rbuf.at[slot, j], sem_ref.at[3, slot])]

    def start_page(bb, ci, slot, j):
        for cp in page_copies(bb, ci, slot, j):
            cp.start()

    def wait_chunk(bb, ci, slot):
        for j in range(g_pages):
            for cp in page_copies(bb, ci, slot, j):
                cp.wait()

    @pl.when(b == 0)
    def _():
        for j in range(g_pages):
            start_page(0, 0, 0, j)
        for h in range(N_HEADS):
            qn = qcat_ref[:, h * MLA_SLOT:h * MLA_SLOT + HEAD_DIM].astype(BF16)
            qabs_ref[h] = _dot_nt(qn, wuk_ref[h])

    row8 = lax.broadcasted_iota(jnp.int32, (SUBLANES, gw), 0)
    lane8 = lax.broadcasted_iota(jnp.int32, (SUBLANES, gw), 1)
    head_mask = row8 == lane8 // HEAD_DIM
    qf8 = jnp.where(head_mask, jnp.broadcast_to(fq_ref[pl.ds(b, 1), :], (SUBLANES, gw)),
                    0.0).astype(BF16)
    pick = (lax.broadcasted_iota(jnp.int32, (nb_static, page), 0) == b).astype(BF16)
    qbc = _dot(fqt_ref[...].astype(BF16), pick).reshape(N_HEADS, HEAD_DIM, page)
    rowk = lax.broadcasted_iota(jnp.int32, (SUBLANES, MLA_KV_RANK), 0)
    qabs8 = jnp.zeros((SUBLANES, MLA_KV_RANK), F32)
    rowr = lax.broadcasted_iota(jnp.int32, (SUBLANES, MLA_ROPE), 0)
    qr8 = jnp.zeros((SUBLANES, MLA_ROPE), F32)
    qrow = qcat_ref[pl.ds(b, 1), :]
    for h in range(N_HEADS):
        qabs8 = jnp.where(rowk == N_HEADS + h,
                          jnp.broadcast_to(qabs_ref[h, pl.ds(b, 1), :], (SUBLANES, MLA_KV_RANK)), qabs8)
        qr_h = qrow[:, h * MLA_SLOT + HEAD_DIM:h * MLA_SLOT + HEAD_DIM + MLA_ROPE]
        qr8 = jnp.where(rowr == N_HEADS + h, jnp.broadcast_to(qr_h, (SUBLANES, MLA_ROPE)), qr8)
    qabs8 = qabs8.astype(BF16)
    qr8 = qr8.astype(BF16)
    cnew =cnew_ref[pl.ds(b, 1), :]
    eye8 = (lax.broadcasted_iota(jnp.int32, (SUBLANES, SUBLANES), 0)
            == lax.broadcasted_iota(jnp.int32, (SUBLANES, SUBLANES), 1))
    cnew_col = jnp.sum(jnp.where(eye8, jnp.broadcast_to(cnew, (SUBLANES, SUBLANES)), 0.0),
                       axis=-1, keepdims=True)

    def chunk(ci, carry):
        m, l, acc_f, acc_m = carry
        gidx = b * nc + ci
        slot = gidx % 2
        gnext = jnp.minimum(gidx + 1, nb * nc - 1)
        nxt_b = gnext // nc
        nxt_c = gnext % nc

        wait_chunk(b, ci, slot)

        s_list = []
        zero_rows = jnp.zeros((SUBLANES - N_HEADS, page), F32)
        for j in range(g_pages):
            start_page(nxt_b, nxt_c, 1 - slot, j)
            s_f = jnp.sum(kbuf[slot, j] * qbc, axis=1)
            cp_ = cbuf[slot, j].astype(BF16)
            rt = rbuf[slot, j].astype(BF16)
            s = jnp.concatenate([s_f, zero_rows], axis=0) + _dot_nt(qabs8, cp_) + _dot(qr8, rt)
            s_list.append(s + bias_ref[ci * g_pages + j] + cnew_col)
        s = jnp.concatenate(s_list, axis=-1)
        m_new = jnp.maximum(m, jnp.max(s, axis=-1, keepdims=True))
        alpha = jnp.exp(m - m_new)
        p = jnp.exp(s - m_new)
        l = alpha * l + jnp.sum(p, axis=-1, keepdims=True)
        pb = p.astype(BF16)
        acc_f = acc_f * alpha[0:N_HEADS].reshape(N_HEADS, 1, 1)
        om = jnp.zeros((SUBLANES, MLA_KV_RANK), F32)
        for j in range(g_pages):
            pj = p[0:N_HEADS, j * page:(j + 1) * page]
            acc_f = acc_f + vbuf[slot, j] * pj[:, None, :]
            om = om + _dot(pb[:, j * page:(j + 1) * page], cbuf[slot, j].astype(BF16))
        return m_new, l, acc_f, alpha * acc_m + om

    init = (jnp.full((SUBLANES, 1), -jnp.inf, F32), jnp.zeros((SUBLANES, 1), F32),
            jnp.zeros((N_HEADS, HEAD_DIM, page), F32), jnp.zeros((SUBLANES, MLA_KV_RANK), F32))
    m, l, acc_f, acc_m = lax.fori_loop(0, nc, chunk, init)
    acc_row = lax.dot_general(jnp.ones((SUBLANES, page), F32), acc_f.reshape(gw, page),
                              (((1,), (1,)), ((), ())), precision=HIGHEST,
                              preferred_element_type=F32)[0:1, :]

    def head_row(col):
        return jnp.sum(jnp.where(head_mask, col, 0.0), axis=0, keepdims=True)

    knew = knew_ref[pl.ds(b, 1), :].astype(BF16).astype(F32)
    vnew = vnew_ref[pl.ds(b, 1), :].astype(BF16).astype(F32)
    ckvnew = ckvnew_ref[pl.ds(b, 1), :].astype(BF16).astype(F32)
    krnew = krnew_ref[pl.ds(b, 1), :].astype(BF16).astype(F32)
    s_new = (jnp.sum(qf8.astype(F32) * knew, axis=-1, keepdims=True)
             + jnp.sum(qabs8.astype(F32) * ckvnew, axis=-1, keepdims=True)
             + jnp.sum(qr8.astype(F32) * krnew, axis=-1, keepdims=True))
    m_fin = jnp.maximum(m, s_new)
    alpha = jnp.exp(m - m_fin)
    p_new = jnp.exp(s_new - m_fin)
    l = alpha * l + p_new
    p_new = p_new.astype(BF16).astype(F32)
    acc_m = alpha * acc_m + p_new * ckvnew
    inv = 1.0 / l
    fox_ref[pl.ds(b, 1), :] = (head_row(alpha) * acc_row + head_row(p_new) * vnew) * head_row(inv)
    olat = acc_m * inv
    for h in range(N_HEADS):
        olat_ref[h, pl.ds(b, 1), :] = olat[N_HEADS + h:N_HEADS + h + 1, :]

    @pl.when(b == nb - 1)
    def _():
        wait_chunk(nb - 1, nc - 1, (nb * nc) % 2)
        for h in range(N_HEADS):
            mla_ref[:, h * HEAD_DIM:(h + 1) * HEAD_DIM] = _dot(olat_ref[h].astype(BF16), wuv_ref[h])


def _decode_call(page_table, bias, fq, qcat, cnew8, knew, vnew, ckvnew, krnew, wuk_h, wuv_h,
                 k_pool, v_pool, ckv_pool, kr_pool, layer):
    nb, n_pages = page_table.shape
    page = k_pool.shape[-1]
    g_pages = math.gcd(n_pages, DECODE_PAGES_PER_STEP)
    gw = GROUP_WIDTH
    vm = [fq, fq.T, qcat, cnew8, knew, vnew, ckvnew, krnew, wuk_h, wuv_h]
    in_specs = ([pl.BlockSpec((None, n_pages, SUBLANES, page), lambda b, pt: (b, 0, 0, 0))]
                + [pl.BlockSpec(a.shape, lambda b, pt, _n=a.ndim: (0,) * _n) for a in vm]
                + [pl.BlockSpec(memory_space=pl.ANY)] * 4)
    out_full = pl.BlockSpec((nb, gw), lambda b, pt: (0, 0))
    return pl.pallas_call(
        functools.partial(_decode_kernel, g_pages=g_pages, layer=layer),
        out_shape=[jax.ShapeDtypeStruct((nb, gw), F32), jax.ShapeDtypeStruct((nb, gw), F32)],
        grid_spec=pltpu.PrefetchScalarGridSpec(
            num_scalar_prefetch=1,
            grid=(nb,),
            in_specs=in_specs,
            out_specs=[out_full, out_full],
            scratch_shapes=[pltpu.VMEM((2, g_pages, N_HEADS, HEAD_DIM, page), F32),
                            pltpu.VMEM((2, g_pages, N_HEADS, HEAD_DIM, page), F32),
                            pltpu.VMEM((2, g_pages, page, MLA_KV_RANK), F32),
                            pltpu.VMEM((2, g_pages, MLA_ROPE, page), F32),
                            pltpu.SemaphoreType.DMA((4, 2)),
                            pltpu.VMEM((N_HEADS, nb, MLA_KV_RANK), F32),
                            pltpu.VMEM((N_HEADS, nb, MLA_KV_RANK), F32)]),
        compiler_params=pltpu.CompilerParams(
            dimension_semantics=("arbitrary",), vmem_limit_bytes=VMEM_LIMIT_BYTES),
        name="decode_attn",
    )(page_table, bias, *vm, k_pool, v_pool, ckv_pool, kr_pool)


def _swap_halves(w):
    half = w.shape[-1] // 2
    return jnp.concatenate([w[..., half:], w[..., :half]], axis=-1)


def _layer_weights(p, l):
    gw = GROUP_WIDTH
    w_in = p['w_in'][l]
    sizes = (gw, gw, gw, N_HEADS, gw, gw, gw, gw, gw, GLA_LOWRANK, MLA_Q_RANK, MLA_KV_RANK, MLA_ROPE)
    offs = np.concatenate([[0], np.cumsum(sizes)])
    fq, fk, fv, ff, su, gq, gk, gv, gg, glr, mq, mkv, mkr = [
        w_in[:, int(offs[i]):int(offs[i + 1])] for i in range(len(sizes))]
    d = w_in.shape[0]
    small = jnp.concatenate([mkr, _swap_halves(mkr), glr, ff,
                             jnp.zeros((d, LANES - 2 * MLA_ROPE - GLA_LOWRANK - N_HEADS), F32)], axis=1)
    wcat = jnp.concatenate([fq, fk, fv, su, gq, gk, gv, gg, mkv, small, mq,
                            jnp.zeros((d, N_ZCOLS - C_MQ - MLA_Q_RANK), F32)], axis=1).astype(BF16)

    w_uq = p['w_mla_uq'][l].reshape(MLA_Q_RANK, N_HEADS, HEAD_DIM + MLA_ROPE)
    nope, ropew = w_uq[..., :HEAD_DIM], w_uq[..., HEAD_DIM:]
    zpad = jnp.zeros((MLA_Q_RANK, N_HEADS, MLA_SLOT - HEAD_DIM - MLA_ROPE), F32)
    w_uq_a = jnp.concatenate([nope, ropew, zpad], axis=-1).reshape(MLA_Q_RANK, N_HEADS * MLA_SLOT)
    w_uq_b = jnp.concatenate([jnp.zeros_like(nope), _swap_halves(ropew), zpad],
                             axis=-1).reshape(MLA_Q_RANK, N_HEADS * MLA_SLOT)
    w_uk = p['w_mla_uk'][l]
    w_uk_slots = jnp.concatenate(
        [w_uk, jnp.zeros((MLA_KV_RANK, N_HEADS, MLA_SLOT - HEAD_DIM), F32)],
        axis=-1).reshape(MLA_KV_RANK, N_HEADS * MLA_SLOT)
    p_kr = jnp.concatenate([jnp.zeros((MLA_ROPE, HEAD_DIM), F32), jnp.eye(MLA_ROPE, dtype=F32),
                            jnp.zeros((MLA_ROPE, MLA_SLOT - HEAD_DIM - MLA_ROPE), F32)], axis=1)
    p_kr = jnp.tile(p_kr, (1, N_HEADS))

    abr, abi, bin_re, bin_im = _s5_discretize(p['s5_a_re'][l], p['s5_a_im'][l], p['s5_log_dt'][l],
                                              p['s5_b_re'][l], p['s5_b_im'][l])
    row = lambda a: a.reshape(1, -1).astype(F32)
    return dict(
        wcat=wcat, norm_mix_pre=row(p['norm_mix_pre'][l]), b_fox_f=row(p['b_fox_f'][l]),
        w_gla_gk=p['w_gla_gk'][l].astype(BF16), b_gla_gk=row(p['b_gla_gk'][l]),
        mla_q_norm=row(p['mla_q_norm'][l]), w_uq_a=w_uq_a.astype(BF16), w_uq_b=w_uq_b.astype(BF16),
        mla_kv_norm=row(p['mla_kv_norm'][l]), w_uk_slots=w_uk_slots.astype(BF16),
        w_uv=p['w_mla_uv'][l].reshape(MLA_KV_RANK, gw).astype(BF16), p_kr=p_kr.astype(BF16),
        w_uk_h=jnp.transpose(w_uk, (1, 0, 2)).astype(BF16),
        w_uv_h=jnp.transpose(p['w_mla_uv'][l], (1, 0, 2)).astype(BF16),
        s5_abar_re=abr, s5_abar_im=abi, s5_bin_re=bin_re, s5_bin_im=bin_im,
        s5_cout_re=_s5_out_blockdiag(p['s5_c_re'][l]), s5_cout_im=_s5_out_blockdiag(p['s5_c_im'][l]),
        s5_d=row(p['s5_d'][l]), w_s5_glu=p['w_s5_glu'][l].astype(BF16), b_s5_glu=row(p['b_s5_glu'][l]),
        gla_norm=row(p['gla_norm'][l]),
        w_out=p['w_out'][l].astype(BF16), norm_mix_post=row(p['norm_mix_post'][l]),
        norm_ffn_pre=row(p['norm_ffn_pre'][l]), norm_ffn_post=row(p['norm_ffn_post'][l]),
        w_ffn_gate=p['w_ffn_gate'][l].astype(BF16), w_ffn_up=p['w_ffn_up'][l].astype(BF16),
        w_ffn_down=p['w_ffn_down'][l].astype(BF16))


def _rope_tables(pos):
    half = MLA_ROPE // 2
    inv = ROPE_THETA ** (-jnp.arange(half, dtype=F32) * 2.0 / MLA_ROPE)
    ang = pos.astype(F32)[:, None] * inv[None, :]
    cos, sin = jnp.cos(ang), jnp.sin(ang)
    n = pos.shape[0]
    c32 = jnp.concatenate([cos, cos], axis=-1)
    s32 = jnp.concatenate([-sin, sin], axis=-1)
    ones = jnp.ones((n, HEAD_DIM), F32)
    zeros = jnp.zeros((n, HEAD_DIM), F32)
    zpad = jnp.zeros((n, MLA_SLOT - HEAD_DIM - MLA_ROPE), F32)
    tqc = jnp.tile(jnp.concatenate([ones, c32, zpad], axis=-1), (1, N_HEADS))
    tqs = jnp.tile(jnp.concatenate([zeros, s32, zpad], axis=-1), (1, N_HEADS))
    return tqc, tqs, jnp.concatenate([c32, s32], axis=-1)


def kernel(x_prompt, x_sample, cache_fox_k, cache_fox_v, cache_fox_logf, cache_mla_ckv, cache_mla_krope, state_s5_re, state_s5_im, state_gla, page_table, meta_tokens, norm_mix_pre, norm_mix_post, norm_ffn_pre, norm_ffn_post, w_in, b_fox_f, s5_a_re, s5_a_im, s5_log_dt, s5_b_re, s5_b_im, s5_c_re, s5_c_im, s5_d, w_s5_glu, b_s5_glu, w_gla_gk, b_gla_gk, gla_norm, mla_q_norm, w_mla_uq, mla_kv_norm, w_mla_uk, w_mla_uv, w_out, w_ffn_gate, w_ffn_up, w_ffn_down):
    params = dict(
        norm_mix_pre=norm_mix_pre, norm_mix_post=norm_mix_post, norm_ffn_pre=norm_ffn_pre,
        norm_ffn_post=norm_ffn_post, w_in=w_in, b_fox_f=b_fox_f, s5_a_re=s5_a_re, s5_a_im=s5_a_im,
        s5_log_dt=s5_log_dt, s5_b_re=s5_b_re, s5_b_im=s5_b_im, s5_c_re=s5_c_re, s5_c_im=s5_c_im,
        s5_d=s5_d, w_s5_glu=w_s5_glu, b_s5_glu=b_s5_glu, w_gla_gk=w_gla_gk, b_gla_gk=b_gla_gk,
        gla_norm=gla_norm, mla_q_norm=mla_q_norm, w_mla_uq=w_mla_uq, mla_kv_norm=mla_kv_norm,
        w_mla_uk=w_mla_uk, w_mla_uv=w_mla_uv, w_out=w_out, w_ffn_gate=w_ffn_gate,
        w_ffn_up=w_ffn_up, w_ffn_down=w_ffn_down)
    depth = w_in.shape[0]
    nbp, seq, d = x_prompt.shape
    nbs, dec_seq, _ = x_sample.shape
    assert dec_seq == 1, "the sample path handles one new token per sequence"
    t = seq + N_META
    n_pages, page = page_table.shape[1], cache_fox_k.shape[2]
    past_len = n_pages * page
    gw = GROUP_WIDTH

    hp = jnp.concatenate([jnp.broadcast_to(meta_tokens.astype(x_prompt.dtype)[None], (nbp, N_META, d)),
                          x_prompt], axis=1)
    hs = x_sample.reshape(1, nbs, d)
    tabs_p = _rope_tables(jnp.arange(t))
    tabs_s = _rope_tables(jnp.full((nbs,), past_len, jnp.int32))
    k_pool_t = jnp.transpose(cache_fox_k, (0, 1, 3, 4, 2))
    v_pool_t = jnp.transpose(cache_fox_v, (0, 1, 3, 4, 2))
    kr_pool_t = jnp.transpose(cache_mla_krope, (0, 1, 3, 2))

    rows_p, rows_s = [], []
    for l in range(depth):
        lw = _layer_weights(params, l)

        (fq, fk, fv, fkb, fvb, logf, c, su_tm, gla_in, qcat, kcat, vmla, ckv, kr) = _proj_call(
            hp, lw, tabs_p, cumsum=True)
        fox_o = _attn_call(fq, fkb, fvb, c)
        mla_o = _attn_call(qcat, kcat, vmla)
        zeros_state = jnp.zeros((nbp, S5_LANES), F32)
        s5_o, s5_re, s5_im = _s5_call(su_tm.reshape(t * nbp, gw), zeros_state, zeros_state, lw, nbp)
        gla_o, gla_st = _gla_call(gla_in, jnp.zeros((nbp, gw, HEAD_DIM), F32), lw['gla_norm'])
        hp = _finish_call(hp, fox_o, s5_o.reshape(t, nbp * gw), gla_o, mla_o, lw)
        rows_p.append((fk.reshape(nbp, t, N_HEADS, HEAD_DIM), fv.reshape(nbp, t, N_HEADS, HEAD_DIM),
                       logf, ckv, kr, s5_re.reshape(nbp, S5_GROUPS, S5_STATE),
                       s5_im.reshape(nbp, S5_GROUPS, S5_STATE),
                       jnp.swapaxes(gla_st.reshape(nbp, N_HEADS, HEAD_DIM, HEAD_DIM), 2, 3)))

        (fq, fk, fv, fkb, fvb, logf, c, su_tm, gla_in, qcat, kcat, vmla, ckv, kr) = _proj_call(
            hs, lw, tabs_s, cumsum=False)
        logf_t = jnp.pad(jnp.swapaxes(cache_fox_logf[l].astype(F32), 1, 2),
                         ((0, 0), (0, SUBLANES - N_HEADS), (0, 0)))
        bias = _fox_bias_call(page_table, logf_t)
        cnew8 = jnp.concatenate([logf[0], jnp.zeros((nbs, SUBLANES - N_HEADS), F32)], axis=1)
        fox_o, mla_o = _decode_call(
            page_table, bias, fq[0].astype(F32), qcat[0].astype(F32), cnew8, fk[0], fv[0], ckv[0], kr[0],
            lw['w_uk_h'], lw['w_uv_h'], k_pool_t, v_pool_t, cache_mla_ckv, kr_pool_t, l)
        s5_o, s5_re, s5_im = _s5_call(su_tm, state_s5_re[l].reshape(nbs, S5_LANES),
                                      state_s5_im[l].reshape(nbs, S5_LANES), lw, nbs)
        gla_pad = jnp.concatenate([jnp.zeros((nbs, SUBLANES - 1, 5 * gw), F32),
                                   gla_in.reshape(nbs, 1, 5 * gw)], axis=1)
        s0t = jnp.swapaxes(state_gla[l].astype(F32), 2, 3).reshape(nbs, gw, HEAD_DIM)
        gla_o, gla_st = _gla_call(gla_pad, s0t, lw['gla_norm'])
        gla_o = gla_o[:, SUBLANES - 1, :].reshape(1, nbs, gw)
        hs = _finish_call(hs, fox_o.reshape(1, nbs, gw), s5_o, gla_o, mla_o.reshape(1, nbs, gw), lw)
        rows_s.append((fk.reshape(nbs, 1, N_HEADS, HEAD_DIM), fv.reshape(nbs, 1, N_HEADS, HEAD_DIM),
                       logf.reshape(nbs, 1, N_HEADS), ckv.reshape(nbs, 1, MLA_KV_RANK),
                       kr.reshape(nbs, 1, MLA_ROPE), s5_re.reshape(nbs, S5_GROUPS, S5_STATE),
                       s5_im.reshape(nbs, S5_GROUPS, S5_STATE),
                       jnp.swapaxes(gla_st.reshape(nbs, N_HEADS, HEAD_DIM, HEAD_DIM), 2, 3)))

    fk_p, fv_p, flf_p, ckv_p, kr_p, s5re_p, s5im_p, gla_p = [jnp.stack(r) for r in zip(*rows_p)]
    fk_s, fv_s, flf_s, ckv_s, kr_s, s5re_s, s5im_s, gla_s = [jnp.stack(r) for r in zip(*rows_s)]
    y_prompt = hp[:, N_META:]
    y_sample = hs.reshape(nbs, 1, d)
    return (y_prompt, y_sample, fk_p, fk_s, fv_p, fv_s, flf_p, flf_s, ckv_p, ckv_s, kr_p, kr_s,
            s5re_p, s5re_s, s5im_p, s5im_s, gla_p, gla_s)
```

```python
import functools
import math

import jax
import jax.numpy as jnp
import numpy as np
from jax import lax
from jax.experimental import pallas as pl
from jax.experimental.pallas import tpu as pltpu

F32 = jnp.float32
BF16 = jnp.bfloat16
HIGHEST = lax.Precision.HIGHEST

N_META = 16
HEAD_DIM = 64
NORM_EPS = 1e-6
N_HEADS = 4
GROUP_WIDTH = N_HEADS * HEAD_DIM
S5_GC = 16
S5_GROUPS = GROUP_WIDTH // S5_GC
S5_STATE = 64
S5_LANES = S5_GROUPS * S5_STATE
GLA_LOWRANK = 16
GLA_GATE_NORM = 16.0
MLA_ROPE = HEAD_DIM // 2
MLA_Q_RANK = 3 * GROUP_WIDTH // 4
MLA_KV_RANK = GROUP_WIDTH // 2
FOX_SCALE = HEAD_DIM ** -0.5
GLA_SCALE = HEAD_DIM ** -0.5
MLA_SCALE = (HEAD_DIM + MLA_ROPE) ** -0.5
ROPE_THETA = 10000.0
MLA_SLOT = 128

SUBLANES = 8
LANES = 128
VMEM_LIMIT_BYTES = 56 * 1024 * 1024

PROJ_TILE = 688
ATTN_TILE = 384
S5_CHUNK = 48
GLA_CHUNK = 48
GLA_BATCH_BLOCK = 8
FFN_TILE = 344
DECODE_PAGES_PER_STEP = 16
DECODE_RING = 3

C_FOX = 0
C_S5 = 768
C_GLA = 1024
C_MKV = 2048
C_SMALL = 2176
C_MQ = 2304
N_ZCOLS = 2560


def _pick_tile(n, target):
    best = None
    for t in range(SUBLANES, min(n, target) + 1, SUBLANES):
        if n % t == 0:
            best = t
    return best if best is not None else n


def _rms(x, w):
    return x * lax.rsqrt(jnp.mean(x * x, axis=-1, keepdims=True) + NORM_EPS) * w


def _log_sigmoid(x):
    return jnp.minimum(x, 0.0) - jnp.log1p(jnp.exp(-jnp.abs(x)))


def _sigmoid(x):
    return 1.0 / (1.0 + jnp.exp(-x))


def _dot(a, b):
    return jnp.dot(a, b, preferred_element_type=F32)


def _dot_nt(a, b):
    return lax.dot_general(a, b, (((1,), (1,)), ((), ())), preferred_element_type=F32)


def _dot_tn(a, b):
    return lax.dot_general(a, b, (((0,), (0,)), ((), ())), preferred_element_type=F32)


def _bf16_pieces(x):
    hi = x.astype(BF16)
    r1 = x - hi.astype(F32)
    mid = r1.astype(BF16)
    lo = (r1 - mid.astype(F32)).astype(BF16)
    return hi, mid, lo


def _dot_exact_lhs(a01, x):
    hi, mid, lo = _bf16_pieces(x)
    return _dot(a01, hi) + _dot(a01, mid) + _dot(a01, lo)


def _const_spec(shape):
    nd = len(shape)
    return pl.BlockSpec(shape, lambda *_: (0,) * nd)


def _proj_kernel(x_ref, nw_ref, w_ref, bff_ref, wgk_ref, bgk_ref, qnw_ref, wuqa_ref, wuqb_ref,
                 kvnw_ref, wuk_ref, wuv_ref, pkr_ref, tqc_ref, tqs_ref, tkr_ref,
                 fq_ref, fk_ref, fv_ref, fkb_ref, fvb_ref, logf_ref, c_ref, su_ref, gla_ref,
                 qcat_ref, kcat_ref, vmla_ref, ckv_ref, kr_ref, carry_ref, *, cumsum):
    tm = x_ref.shape[0]
    hn = _rms(x_ref[...], nw_ref[...]).astype(BF16)

    zf = _dot(hn, w_ref[:, C_FOX:C_FOX + 3 * GROUP_WIDTH])
    fq_ref[...] = (zf[:, :GROUP_WIDTH] * FOX_SCALE).astype(BF16)
    fk = zf[:, GROUP_WIDTH:2 * GROUP_WIDTH]
    fv = zf[:, 2 * GROUP_WIDTH:]
    fk_ref[...] = fk
    fv_ref[...] = fv
    fkb_ref[...] = fk.astype(BF16)
    fvb_ref[...] = fv.astype(BF16)

    su_ref[...] = _dot(hn, w_ref[:, C_S5:C_S5 + GROUP_WIDTH])

    zs = _dot(hn, w_ref[:, C_SMALL:C_SMALL + LANES])
    mkr = zs[:, 0:MLA_ROPE]
    mkr_sw = zs[:, MLA_ROPE:2 * MLA_ROPE]
    glr = zs[:, 2 * MLA_ROPE:2 * MLA_ROPE + GLA_LOWRANK]
    ff = zs[:, 2 * MLA_ROPE + GLA_LOWRANK:2 * MLA_ROPE + GLA_LOWRANK + N_HEADS]

    logf = _log_sigmoid(ff + bff_ref[...])
    logf_ref[...] = logf
    if cumsum:
        ti = pl.program_id(1)

        @pl.when(ti == 0)
        def _():
            carry_ref[...] = jnp.zeros_like(carry_ref)

        row = lax.broadcasted_iota(jnp.int32, (tm, tm), 0)
        col = lax.broadcasted_iota(jnp.int32, (tm, tm), 1)
        tri = jnp.where(row >= col, 1.0, 0.0).astype(BF16)
        c = _dot_exact_lhs(tri, logf) + carry_ref[...]
        c_ref[...] = c
        carry_ref[...] = c[tm - 1:tm, :]
    else:
        c_ref[...] = logf

    tkr = tkr_ref[...]
    kr = mkr * tkr[:, :MLA_ROPE] + mkr_sw * tkr[:, MLA_ROPE:]
    kr_ref[...] = kr

    glog = _log_sigmoid(_dot(glr.astype(BF16), wgk_ref[...]) + bgk_ref[...]) * (1.0 / GLA_GATE_NORM)
    zg = _dot(hn, w_ref[:, C_GLA:C_GLA + 4 * GROUP_WIDTH])
    gla_ref[:, 0:GROUP_WIDTH] = zg[:, 0:GROUP_WIDTH] * GLA_SCALE
    gla_ref[:, GROUP_WIDTH:3 * GROUP_WIDTH] = zg[:, GROUP_WIDTH:3 * GROUP_WIDTH]
    gla_ref[:, 3 * GROUP_WIDTH:4 * GROUP_WIDTH] = glog
    gla_ref[:, 4 * GROUP_WIDTH:5 * GROUP_WIDTH] = zg[:, 3 * GROUP_WIDTH:]

    mkv = _dot(hn, w_ref[:, C_MKV:C_MKV + MLA_KV_RANK])
    ckv = _rms(mkv, kvnw_ref[...])
    ckv_ref[...] = ckv
    ckv_b = ckv.astype(BF16)
    kcat = _dot(ckv_b, wuk_ref[...]) + _dot(kr.astype(BF16), pkr_ref[...])
    kcat_ref[...] = kcat.astype(BF16)
    vmla_ref[...] = _dot(ckv_b, wuv_ref[...]).astype(BF16)

    mq = _dot(hn, w_ref[:, C_MQ:C_MQ + MLA_Q_RANK])
    qln = _rms(mq, qnw_ref[...]).astype(BF16)
    qcat = _dot(qln, wuqa_ref[...]) * tqc_ref[...] + _dot(qln, wuqb_ref[...]) * tqs_ref[...]
    qcat_ref[...] = (qcat * MLA_SCALE).astype(BF16)


def _proj_call(x3, lw, tabs, *, cumsum):
    nb, t, d = x3.shape
    tm = _pick_tile(t, PROJ_TILE)
    nt = t // tm
    tqc, tqs, tkr = tabs
    gw = GROUP_WIDTH
    qs = N_HEADS * MLA_SLOT

    def tok(width):
        return pl.BlockSpec((None, tm, width), lambda b, i: (b, i, 0))

    def tab(width):
        return pl.BlockSpec((tm, width), lambda b, i: (i, 0))

    consts = [lw['norm_mix_pre'], lw['wcat'], lw['b_fox_f'], lw['w_gla_gk'], lw['b_gla_gk'],
              lw['mla_q_norm'], lw['w_uq_a'], lw['w_uq_b'], lw['mla_kv_norm'], lw['w_uk_slots'],
              lw['w_uv'], lw['p_kr']]
    in_specs = [tok(d)] + [_const_spec(c.shape) for c in consts] + [tab(qs), tab(qs), tab(2 * MLA_ROPE)]
    out_shape = [
        jax.ShapeDtypeStruct((nb, t, gw), BF16),
        jax.ShapeDtypeStruct((nb, t, gw), F32),
        jax.ShapeDtypeStruct((nb, t, gw), F32),
        jax.ShapeDtypeStruct((nb, t, gw), BF16),
        jax.ShapeDtypeStruct((nb, t, gw), BF16),
        jax.ShapeDtypeStruct((nb, t, N_HEADS), F32),
        jax.ShapeDtypeStruct((nb, t, N_HEADS), F32),
        jax.ShapeDtypeStruct((t, nb * gw), F32),
        jax.ShapeDtypeStruct((nb, t, 5 * gw), F32),
        jax.ShapeDtypeStruct((nb, t, qs), BF16),
        jax.ShapeDtypeStruct((nb, t, qs), BF16),
        jax.ShapeDtypeStruct((nb, t, gw), BF16),
        jax.ShapeDtypeStruct((nb, t, MLA_KV_RANK), F32),
        jax.ShapeDtypeStruct((nb, t, MLA_ROPE), F32),
    ]
    out_specs = [tok(gw), tok(gw), tok(gw), tok(gw), tok(gw), tok(N_HEADS), tok(N_HEADS),
                 pl.BlockSpec((tm, gw), lambda b, i: (i, b)),
                 tok(5 * gw), tok(qs), tok(qs), tok(gw), tok(MLA_KV_RANK), tok(MLA_ROPE)]
    return pl.pallas_call(
        functools.partial(_proj_kernel, cumsum=cumsum),
        out_shape=out_shape,
        grid=(nb, nt),
        in_specs=in_specs,
        out_specs=out_specs,
        scratch_shapes=[pltpu.VMEM((1, N_HEADS), F32)],
        compiler_params=pltpu.CompilerParams(
            dimension_semantics=("arbitrary", "arbitrary"), vmem_limit_bytes=VMEM_LIMIT_BYTES),
        name="proj",
    )(x3, *consts, tqc, tqs, tkr)


def _attn_kernel(*refs, dk, has_bias):
    if has_bias:
        q_ref, k_ref, v_ref, c_ref, o_ref, qp_ref, kp_ref, vp_ref, s_ref, cp_ref = refs
    else:
        q_ref, k_ref, v_ref, o_ref, qp_ref, kp_ref, vp_ref, s_ref = refs
    t = k_ref.shape[0]
    tq = o_ref.shape[0]
    tp = kp_ref.shape[0]
    qi = pl.program_id(1)

    @pl.when(qi == 0)
    def _():
        pairs = [(q_ref, qp_ref), (k_ref, kp_ref), (v_ref, vp_ref)]
        if has_bias:
            pairs.append((c_ref, cp_ref))
        for src, dst in pairs:
            dst[0:t, :] = src[...]
            dst[t:tp, :] = jnp.zeros((tp - t, dst.shape[1]), dst.dtype)

    q0 = pl.multiple_of(qi * tq, tq)
    causal = (lax.broadcasted_iota(jnp.int32, (tq, tq), 0)
              <= lax.broadcasted_iota(jnp.int32, (tq, tq), 1))

    qs = [qp_ref[pl.ds(q0, tq), h * dk:(h + 1) * dk] for h in range(N_HEADS)]

    groups = tq // SUBLANES

    def fold(x):
        return x.reshape(groups, SUBLANES, tq)

    def score_tile(h, k0, kj, masked, mx):
        s = _dot_nt(kp_ref[pl.ds(k0, tq), h * dk:(h + 1) * dk], qs[h])
        if has_bias:
            s = s - cp_ref[pl.ds(k0, tq), h:h + 1]
        if masked:
            s = jnp.where(causal, s, -jnp.inf)
        s_ref[h, kj] = s
        return jnp.maximum(mx, jnp.max(fold(s), axis=0))

    def pass1(kj, mxs):
        k0 = pl.multiple_of(kj * tq, tq)
        return tuple(score_tile(h, k0, kj, False, mxs[h]) for h in range(N_HEADS))

    mxs = lax.fori_loop(0, qi, pass1,
                        tuple(jnp.full((SUBLANES, tq), -jnp.inf, F32) for _ in range(N_HEADS)))
    ms = [jnp.max(score_tile(h, q0, qi, True, mxs[h]), axis=0, keepdims=True) for h in range(N_HEADS)]

    def pass2(kj, carry):
        k0 = pl.multiple_of(kj * tq, tq)
        new = []
        for h in range(N_HEADS):
            lsum, acc = carry[h]
            p = jnp.exp(s_ref[h, kj] - ms[h])
            v = vp_ref[pl.ds(k0, tq), h * HEAD_DIM:(h + 1) * HEAD_DIM]
            new.append((lsum + jnp.sum(fold(p), axis=0),
                        acc + _dot_tn(v, p.astype(BF16))))
        return tuple(new)

    carry = lax.fori_loop(0, qi + 1, pass2,
                          tuple((jnp.zeros((SUBLANES, tq), F32), jnp.zeros((HEAD_DIM, tq), F32))
                                for _ in range(N_HEADS)))
    outs = [acc / jnp.sum(lsum, axis=0, keepdims=True) for (lsum, acc) in carry]
    for pair in range(N_HEADS // 2):
        o_ref[:, pair * 2 * HEAD_DIM:(pair + 1) * 2 * HEAD_DIM] = jnp.concatenate(
            outs[2 * pair:2 * pair + 2], axis=0).T


def _attn_call(q, k, v, c=None):
    nb, t, qw = q.shape
    dk = qw // N_HEADS
    tq = ATTN_TILE
    nq = pl.cdiv(t, tq)
    tp = nq * tq
    has_bias = c is not None

    def seq(width):
        return pl.BlockSpec((None, t, width), lambda b, i: (b, 0, 0))

    in_specs = [seq(qw), seq(qw), seq(GROUP_WIDTH)]
    args = [q, k, v]
    scratch = [pltpu.VMEM((tp, qw), BF16), pltpu.VMEM((tp, qw), BF16), pltpu.VMEM((tp, GROUP_WIDTH), BF16),
               pltpu.VMEM((N_HEADS, nq, tq, tq), F32)]
    if has_bias:
        in_specs.append(seq(N_HEADS))
        args.append(c)
        scratch.append(pltpu.VMEM((tp, N_HEADS), F32))
    return pl.pallas_call(
        functools.partial(_attn_kernel, dk=dk, has_bias=has_bias),
        out_shape=jax.ShapeDtypeStruct((nb, tp, GROUP_WIDTH), F32),
        grid=(nb, nq),
        in_specs=in_specs,
        out_specs=pl.BlockSpec((None, tq, GROUP_WIDTH), lambda b, i: (b, i, 0)),
        scratch_shapes=scratch,
        compiler_params=pltpu.CompilerParams(
            dimension_semantics=("arbitrary", "arbitrary"), vmem_limit_bytes=VMEM_LIMIT_BYTES),
        name="fox_attn" if has_bias else "mla_attn",
    )(*args)


def _s5_disc_kernel(are_ref, aim_ref, ldt_ref, bre_ref, bim_ref,
                    abr_ref, abi_ref, bbr_ref, bbi_ref):
    dt = jnp.exp(ldt_ref[...])
    ar = are_ref[...]
    ai = aim_ref[...]
    mag = jnp.exp(ar * dt)
    abr = mag * jnp.cos(ai * dt)
    abi = mag * jnp.sin(ai * dt)
    nr = abr - 1.0
    ni = abi
    den = ar * ar + ai * ai
    fr = (nr * ar + ni * ai) / den
    fi = (ni * ar - nr * ai) / den
    abr_ref[...] = abr
    abi_ref[...] = abi
    for c in range(S5_GC):
        br = bre_ref[c]
        bi = bim_ref[c]
        bbr_ref[c] = fr * br - fi * bi
        bbi_ref[c] = fr * bi + fi * br


def _s5_discretize(a_re, a_im, log_dt, b_re, b_im):
    g, p = a_re.shape
    bt = lambda b: jnp.transpose(b, (2, 0, 1))
    outs = pl.pallas_call(
        _s5_disc_kernel,
        out_shape=[jax.ShapeDtypeStruct((g, p), F32), jax.ShapeDtypeStruct((g, p), F32),
                   jax.ShapeDtypeStruct((S5_GC, g, p), F32), jax.ShapeDtypeStruct((S5_GC, g, p), F32)],
        name="s5_discretize",
    )(a_re, a_im, log_dt.reshape(g, 1), bt(b_re), bt(b_im))
    abr, abi, bbr, bbi = outs
    eye = jnp.eye(g, dtype=F32)

    def in_blockdiag(bb):
        m = jnp.transpose(bb, (1, 0, 2))[:, :, None, :] * eye[:, None, :, None]
        return m.reshape(g * S5_GC, g * p).astype(BF16)

    return abr.reshape(1, g * p), abi.reshape(1, g * p), in_blockdiag(bbr), in_blockdiag(bbi)


def _s5_out_blockdiag(c):
    g = c.shape[0]
    eye = jnp.eye(g, dtype=F32)
    m = jnp.transpose(c, (0, 2, 1))[:, :, None, :] * eye[:, None, :, None]
    return m.reshape(g * c.shape[2], g * c.shape[1]).astype(BF16)


def _gelu_tanh(x):
    return 0.5 * x * (1.0 + jnp.tanh(math.sqrt(2.0 / math.pi) * (x + 0.044715 * (x * x * x))))


def _s5_kernel(u_ref, h0r_ref, h0i_ref, ar_ref, ai_ref, bre_ref, bim_ref, cre_ref, cim_ref,
               d_ref, wglu_ref, bglu_ref, o_ref, hr_ref, hi_ref, sr_ref, si_ref, *, nb):
    i = pl.program_id(0)
    tc = u_ref.shape[0] // nb

    @pl.when(i == 0)
    def _():
        hr_ref[...] = h0r_ref[...]
        hi_ref[...] = h0i_ref[...]

    u = u_ref[...]
    ub = u.astype(BF16)
    sr_ref[...] = _dot(ub, bre_ref[...])
    si_ref[...] = _dot(ub, bim_ref[...])
    ar = ar_ref[...]
    ai = ai_ref[...]

    def step(t, carry):
        hr, hi = carry
        r0 = pl.multiple_of(t * nb, nb)
        nhr = ar * hr - ai * hi + sr_ref[pl.ds(r0, nb), :]
        nhi = ar * hi + ai * hr + si_ref[pl.ds(r0, nb), :]
        sr_ref[pl.ds(r0, nb), :] = nhr
        si_ref[pl.ds(r0, nb), :] = nhi
        return nhr, nhi

    hr, hi = lax.fori_loop(0, tc, step, (hr_ref[...], hi_ref[...]))
    hr_ref[...] = hr
    hi_ref[...] = hi

    y = (_dot(sr_ref[...].astype(BF16), cre_ref[...]) - _dot(si_ref[...].astype(BF16), cim_ref[...])
         + d_ref[...] * u)
    g = _gelu_tanh(y)
    o_ref[...] = g * _sigmoid(_dot(g.astype(BF16), wglu_ref[...]) + bglu_ref[...])


def _s5_call(u_tm, h0r, h0i, lw, nb):
    rows = u_tm.shape[0]
    t = rows // nb
    tc = _pick_tile(t, S5_CHUNK) if t >= SUBLANES else t
    r = tc * nb
    consts = [lw['s5_abar_re'], lw['s5_abar_im'], lw['s5_bin_re'], lw['s5_bin_im'],
              lw['s5_cout_re'], lw['s5_cout_im'], lw['s5_d'], lw['w_s5_glu'], lw['b_s5_glu']]
    state_spec = _const_spec((nb, S5_LANES))
    return pl.pallas_call(
        functools.partial(_s5_kernel, nb=nb),
        out_shape=[jax.ShapeDtypeStruct((rows, GROUP_WIDTH), F32),
                   jax.ShapeDtypeStruct((nb, S5_LANES), F32),
                   jax.ShapeDtypeStruct((nb, S5_LANES), F32)],
        grid=(t // tc,),
        in_specs=[pl.BlockSpec((r, GROUP_WIDTH), lambda i: (i, 0)), state_spec, state_spec]
                 + [_const_spec(c.shape) for c in consts],
        out_specs=[pl.BlockSpec((r, GROUP_WIDTH), lambda i: (i, 0)), state_spec, state_spec],
        scratch_shapes=[pltpu.VMEM((r, S5_LANES), F32), pltpu.VMEM((r, S5_LANES), F32)],
        compiler_params=pltpu.CompilerParams(
            dimension_semantics=("arbitrary",), vmem_limit_bytes=VMEM_LIMIT_BYTES),
        name="s5_scan",
    )(u_tm, h0r, h0i, *consts)


def _gla_kernel(x_ref, s0_ref, nw_ref, o_ref, st_ref):
    ci = pl.program_id(1)
    bb, c = x_ref.shape[0], x_ref.shape[1]
    gw = GROUP_WIDTH

    @pl.when(ci == 0)
    def _():
        st_ref[...] = s0_ref[...]

    row = lax.broadcasted_iota(jnp.int32, (c, c), 0)
    col = lax.broadcasted_iota(jnp.int32, (c, c), 1)
    causal = row >= col
    tri = jnp.where(causal, 1.0, 0.0).astype(BF16)

    for bi in range(bb):
        q = x_ref[bi, :, 0:gw]
        k = x_ref[bi, :, gw:2 * gw]
        v = x_ref[bi, :, 2 * gw:3 * gw]
        g = x_ref[bi, :, 3 * gw:4 * gw]
        gate = x_ref[bi, :, 4 * gw:5 * gw]
        bc = _dot_exact_lhs(tri, g)
        b_last = bc[c - 1:c, :]
        b_mid = bc[c // 2:c // 2 + 1, :]
        q_in = (q * jnp.exp(bc - b_mid)).astype(BF16)
        k_in = (k * jnp.exp(b_mid - bc)).astype(BF16)
        q_st = (q * jnp.exp(bc)).astype(BF16)
        k_st = (k * jnp.exp(b_last - bc)).astype(BF16)
        vb = v.astype(BF16)
        decay = jnp.exp(b_last)
        for h in range(N_HEADS):
            sl = slice(h * HEAD_DIM, (h + 1) * HEAD_DIM)
            att = jnp.where(causal, _dot_nt(q_in[:, sl], k_in[:, sl]), 0.0)
            st = st_ref[bi, sl, :]
            o = _dot_nt(q_st[:, sl], st.astype(BF16)) + _dot(att.astype(BF16), vb[:, sl])
            st_ref[bi, sl, :] = st * decay[:, sl] + _dot_tn(vb[:, sl], k_st[:, sl])
            o = _rms(o, nw_ref[...])
            gt = gate[:, sl]
            o_ref[bi, :, sl] = o * (gt * _sigmoid(gt))


def _gla_call(gla_in, s0t, norm_w):
    nb, t, w = gla_in.shape
    c = _pick_tile(t, GLA_CHUNK)
    bb = math.gcd(nb, GLA_BATCH_BLOCK)
    return pl.pallas_call(
        _gla_kernel,
        out_shape=[jax.ShapeDtypeStruct((nb, t, GROUP_WIDTH), F32),
                   jax.ShapeDtypeStruct((nb, GROUP_WIDTH, HEAD_DIM), F32)],
        grid=(nb // bb, t // c),
        in_specs=[pl.BlockSpec((bb, c, w), lambda b, i: (b, i, 0)),
                  pl.BlockSpec((bb, GROUP_WIDTH, HEAD_DIM), lambda b, i: (b, 0, 0)),
                  _const_spec(norm_w.shape)],
        out_specs=[pl.BlockSpec((bb, c, GROUP_WIDTH), lambda b, i: (b, i, 0)),
                   pl.BlockSpec((bb, GROUP_WIDTH, HEAD_DIM), lambda b, i: (b, 0, 0))],
        compiler_params=pltpu.CompilerParams(
            dimension_semantics=("arbitrary", "arbitrary"), vmem_limit_bytes=VMEM_LIMIT_BYTES),
        name="gla",
    )(gla_in, s0t, norm_w)


def _finish_kernel(x_ref, fox_ref, s5_ref, gla_ref, mla_ref, wout_ref, npost_ref, npre_ref,
                   nffn_ref, wg_ref, wu_ref, wd_ref, o_ref):
    gw = GROUP_WIDTH
    mix = (_dot(fox_ref[...].astype(BF16), wout_ref[0:gw, :])
           + _dot(s5_ref[...].astype(BF16), wout_ref[gw:2 * gw, :])
           + _dot(gla_ref[...].astype(BF16), wout_ref[2 * gw:3 * gw, :])
           + _dot(mla_ref[...].astype(BF16), wout_ref[3 * gw:4 * gw, :]))
    x1 = x_ref[...] + _rms(mix, npost_ref[...])
    hf = _rms(x1, npre_ref[...]).astype(BF16)
    gate = _dot(hf, wg_ref[...])
    act = (gate * _sigmoid(gate) * _dot(hf, wu_ref[...])).astype(BF16)
    f = _dot(act, wd_ref[...])
    o_ref[...] = x1 + _rms(f, nffn_ref[...])


def _finish_call(x3, fox_o, s5_tm, gla_o, mla_o, lw):
    nb, t, d = x3.shape
    tm = _pick_tile(t, FFN_TILE)
    gw = GROUP_WIDTH

    def tok(width):
        return pl.BlockSpec((None, tm, width), lambda b, i: (b, i, 0))

    consts = [lw['w_out'], lw['norm_mix_post'], lw['norm_ffn_pre'], lw['norm_ffn_post'],
              lw['w_ffn_gate'], lw['w_ffn_up'], lw['w_ffn_down']]
    const_specs = [pl.BlockSpec(c.shape, lambda b, i: (0, 0), pipeline_mode=pl.Buffered(1))
                   for c in consts]
    return pl.pallas_call(
        _finish_kernel,
        out_shape=jax.ShapeDtypeStruct((nb, t, d), F32),
        grid=(nb, t // tm),
        in_specs=[tok(d), tok(gw), pl.BlockSpec((tm, gw), lambda b, i: (i, b)), tok(gw), tok(gw)]
                 + const_specs,
        out_specs=tok(d),
        compiler_params=pltpu.CompilerParams(
            dimension_semantics=("arbitrary", "arbitrary"), vmem_limit_bytes=VMEM_LIMIT_BYTES),
        name="mix_ffn",
    )(x3, fox_o, s5_tm, gla_o, mla_o, *consts)


def _fox_bias_kernel(pt_ref, logf_hbm, o_ref, buf_ref, sem_ref):
    b = pl.program_id(0)
    nb = pl.num_programs(0)
    n_pages = buf_ref.shape[1]
    page = buf_ref.shape[3]
    rows = n_pages * SUBLANES

    def page_copy(bb, slot, p):
        return pltpu.make_async_copy(logf_hbm.at[pt_ref[bb, p]], buf_ref.at[slot, p], sem_ref.at[slot])

    def start_all(bb, slot):
        def go(p, _):
            page_copy(bb, slot, p).start()
            return 0
        lax.fori_loop(0, n_pages, go, 0)

    @pl.when(b == 0)
    def _():
        start_all(0, 0)

    slot = b % 2

    @pl.when(b + 1 < nb)
    def _():
        start_all(b + 1, 1 - slot)

    def wait_one(p, _):
        page_copy(b, slot, p).wait()
        return 0
    lax.fori_loop(0, n_pages, wait_one, 0)

    x = buf_ref[slot].reshape(rows, page)
    hi, mid, lo = _bf16_pieces(x)
    j = lax.broadcasted_iota(jnp.int32, (page, 2 * page), 0)
    s = lax.broadcasted_iota(jnp.int32, (page, 2 * page), 1)
    w = jnp.where((j > s) | (s >= page), 1.0, 0.0).astype(BF16)
    acc = (_dot(hi, w) + _dot(mid, w) + _dot(lo, w)).reshape(n_pages, SUBLANES, 2 * page)
    run = jnp.zeros((SUBLANES, page), F32)
    for p in reversed(range(n_pages)):
        o_ref[p] = acc[p, :, :page] + run
        run = run + acc[p, :, page:]


def _fox_bias_call(page_table, logf_t):
    nb, n_pages = page_table.shape
    page = logf_t.shape[2]
    rows = n_pages * SUBLANES
    return pl.pallas_call(
        _fox_bias_kernel,
        out_shape=jax.ShapeDtypeStruct((nb, n_pages, SUBLANES, page), F32),
        grid_spec=pltpu.PrefetchScalarGridSpec(
            num_scalar_prefetch=1,
            grid=(nb,),
            in_specs=[pl.BlockSpec(memory_space=pl.ANY)],
            out_specs=pl.BlockSpec((None, n_pages, SUBLANES, page), lambda b, pt: (b, 0, 0, 0)),
            scratch_shapes=[pltpu.VMEM((2, n_pages, SUBLANES, page), F32),
                            pltpu.SemaphoreType.DMA((2,))]),
        compiler_params=pltpu.CompilerParams(
            dimension_semantics=("arbitrary",), vmem_limit_bytes=VMEM_LIMIT_BYTES),
        name="fox_decode_bias",
    )(page_table, logf_t)


def _decode_kernel(pt_ref, bias_ref, fq_ref, fqt_ref, qcat_ref, cnew_ref, knew_ref, vnew_ref, ckvnew_ref,
                   krnew_ref, wuk_ref, wuv_ref, k_hbm, v_hbm, ckv_hbm, kr_hbm,
                   fox_ref, mla_ref,
                   kbuf, vbuf, cbuf, rbuf, sem_ref, qabs_ref, olat_ref, *, g_pages, layer):
    b = pl.program_id(0)
    nb = pl.num_programs(0)
    n_pages = bias_ref.shape[0]
    nc = n_pages // g_pages
    page = bias_ref.shape[2]
    gw = GROUP_WIDTH
    nb_static = fq_ref.shape[0]

    def page_copies(bb, ci, slot, j):
        pid = pt_ref[bb, ci * g_pages + j]
        return [pltpu.make_async_copy(k_hbm.at[layer, pid], kbuf.at[slot, j], sem_ref.at[0, slot]),
                pltpu.make_async_copy(v_hbm.at[layer, pid], vbuf.at[slot, j], sem_ref.at[1, slot]),
                pltpu.make_async_copy(ckv_hbm.at[layer, pid], cbuf.at[slot, j], sem_ref.at[2, slot]),
                pltpu.make_async_copy(kr_hbm.at[layer, pid], rbuf.at[slot, j], sem_ref.at[3, slot])]

    def start_page(bb, ci, slot, j):
        for cp in page_copies(bb, ci, slot, j):
            cp.start()

    def wait_chunk(bb, ci, slot):
        for j in range(g_pages):
            for cp in page_copies(bb, ci, slot, j):
                cp.wait()

    total = nb_static * nc
    depth = DECODE_RING - 1

    def chunk_of(g):
        g = jnp.minimum(g, total - 1)
        return g // nc, g % nc

    @pl.when(b == 0)
    def _():
        for g in range(depth):
            for j in range(g_pages):
                start_page(*chunk_of(g), g % DECODE_RING, j)
        for h in range(N_HEADS):
            qn = qcat_ref[:, h * MLA_SLOT:h * MLA_SLOT + HEAD_DIM].astype(BF16)
            qabs_ref[h] = _dot_nt(qn, wuk_ref[h])

    row8 = lax.broadcasted_iota(jnp.int32, (SUBLANES, gw), 0)
    lane8 = lax.broadcasted_iota(jnp.int32, (SUBLANES, gw), 1)
    head_mask = row8 == lane8 // HEAD_DIM
    qf8 = jnp.where(head_mask, jnp.broadcast_to(fq_ref[pl.ds(b, 1), :], (SUBLANES, gw)),
                    0.0).astype(BF16)
    pick = (lax.broadcasted_iota(jnp.int32, (nb_static, page), 0) == b).astype(BF16)
    qbc = _dot(fqt_ref[...].astype(BF16), pick).reshape(N_HEADS, HEAD_DIM, page)
    rowk = lax.broadcasted_iota(jnp.int32, (SUBLANES, MLA_KV_RANK), 0)
    qabs8 = jnp.zeros((SUBLANES, MLA_KV_RANK), F32)
    rowr = lax.broadcasted_iota(jnp.int32, (SUBLANES, MLA_ROPE), 0)
    qr8 = jnp.zeros((SUBLANES, MLA_ROPE), F32)
    qrow = qcat_ref[pl.ds(b, 1), :]
    for h in range(N_HEADS):
        qabs8 = jnp.where(rowk == N_HEADS + h,
                          jnp.broadcast_to(qabs_ref[h, pl.ds(b, 1), :], (SUBLANES, MLA_KV_RANK)), qabs8)
        qr_h = qrow[:, h * MLA_SLOT + HEAD_DIM:h * MLA_SLOT + HEAD_DIM + MLA_ROPE]
        qr8 = jnp.where(rowr == N_HEADS + h, jnp.broadcast_to(qr_h, (SUBLANES, MLA_ROPE)), qr8)
    qabs8 = qabs8.astype(BF16)
    qr8 = qr8.astype(BF16)
    cnew =cnew_ref[pl.ds(b, 1), :]
    eye8 = (lax.broadcasted_iota(jnp.int32, (SUBLANES, SUBLANES), 0)
            == lax.broadcasted_iota(jnp.int32, (SUBLANES, SUBLANES), 1))
    cnew_col = jnp.sum(jnp.where(eye8, jnp.broadcast_to(cnew, (SUBLANES, SUBLANES)), 0.0),
                       axis=-1, keepdims=True)

    def chunk(ci, carry):
        m, l, acc_f, acc_m = carry
        gidx = b * nc + ci
        slot = gidx % DECODE_RING
        nxt_b, nxt_c = chunk_of(gidx + depth)
        nxt_slot = (gidx + depth) % DECODE_RING

        wait_chunk(b, ci, slot)

        s_list = []
        zero_rows = jnp.zeros((SUBLANES - N_HEADS, page), F32)
        for j in range(g_pages):
            start_page(nxt_b, nxt_c, nxt_slot, j)
            s_f = jnp.sum(kbuf[slot, j] * qbc, axis=1)
            cp_ = cbuf[slot, j].astype(BF16)
            rt = rbuf[slot, j].astype(BF16)
            s = jnp.concatenate([s_f, zero_rows], axis=0) + _dot_nt(qabs8, cp_) + _dot(qr8, rt)
            s_list.append(s + bias_ref[ci * g_pages + j] + cnew_col)
        s = jnp.concatenate(s_list, axis=-1)
        m_new = jnp.maximum(m, jnp.max(s, axis=-1, keepdims=True))
        alpha = jnp.exp(m - m_new)
        p = jnp.exp(s - m_new)
        l = alpha * l + jnp.sum(p, axis=-1, keepdims=True)
        pb = p.astype(BF16)
        acc_f = acc_f * alpha[0:N_HEADS].reshape(N_HEADS, 1, 1)
        om = jnp.zeros((SUBLANES, MLA_KV_RANK), F32)
        for j in range(g_pages):
            pj = p[0:N_HEADS, j * page:(j + 1) * page]
            acc_f = acc_f + vbuf[slot, j] * pj[:, None, :]
            om = om + _dot(pb[:, j * page:(j + 1) * page], cbuf[slot, j].astype(BF16))
        return m_new, l, acc_f, alpha * acc_m + om

    init = (jnp.full((SUBLANES, 1), -jnp.inf, F32), jnp.zeros((SUBLANES, 1), F32),
            jnp.zeros((N_HEADS, HEAD_DIM, page), F32), jnp.zeros((SUBLANES, MLA_KV_RANK), F32))
    m, l, acc_f, acc_m = lax.fori_loop(0, nc, chunk, init)
    acc_row = lax.dot_general(jnp.ones((SUBLANES, page), F32), acc_f.reshape(gw, page),
                              (((1,), (1,)), ((), ())), precision=HIGHEST,
                              preferred_element_type=F32)[0:1, :]

    def head_row(col):
        return jnp.sum(jnp.where(head_mask, col, 0.0), axis=0, keepdims=True)

    knew = knew_ref[pl.ds(b, 1), :].astype(BF16).astype(F32)
    vnew = vnew_ref[pl.ds(b, 1), :].astype(BF16).astype(F32)
    ckvnew = ckvnew_ref[pl.ds(b, 1), :].astype(BF16).astype(F32)
    krnew = krnew_ref[pl.ds(b, 1), :].astype(BF16).astype(F32)
    s_new = (jnp.sum(qf8.astype(F32) * knew, axis=-1, keepdims=True)
             + jnp.sum(qabs8.astype(F32) * ckvnew, axis=-1, keepdims=True)
             + jnp.sum(qr8.astype(F32) * krnew, axis=-1, keepdims=True))
    m_fin = jnp.maximum(m, s_new)
    alpha = jnp.exp(m - m_fin)
    p_new = jnp.exp(s_new - m_fin)
    l = alpha * l + p_new
    p_new = p_new.astype(BF16).astype(F32)
    acc_m = alpha * acc_m + p_new * ckvnew
    inv = 1.0 / l
    fox_ref[pl.ds(b, 1), :] = (head_row(alpha) * acc_row + head_row(p_new) * vnew) * head_row(inv)
    olat = acc_m * inv
    for h in range(N_HEADS):
        olat_ref[h, pl.ds(b, 1), :] = olat[N_HEADS + h:N_HEADS + h + 1, :]

    @pl.when(b == nb - 1)
    def _():
        for g in range(depth):
            wait_chunk(nb_static - 1, nc - 1, (total + g) % DECODE_RING)
        for h in range(N_HEADS):
            mla_ref[:, h * HEAD_DIM:(h + 1) * HEAD_DIM] = _dot(olat_ref[h].astype(BF16), wuv_ref[h])


def _decode_call(page_table, bias, fq, qcat, cnew8, knew, vnew, ckvnew, krnew, wuk_h, wuv_h,
                 k_pool, v_pool, ckv_pool, kr_pool, layer):
    nb, n_pages = page_table.shape
    page = k_pool.shape[-1]
    g_pages = math.gcd(n_pages, DECODE_PAGES_PER_STEP)
    gw = GROUP_WIDTH
    vm = [fq, fq.T, qcat, cnew8, knew, vnew, ckvnew, krnew, wuk_h, wuv_h]
    in_specs = ([pl.BlockSpec((None, n_pages, SUBLANES, page), lambda b, pt: (b, 0, 0, 0))]
                + [pl.BlockSpec(a.shape, lambda b, pt, _n=a.ndim: (0,) * _n) for a in vm]
                + [pl.BlockSpec(memory_space=pl.ANY)] * 4)
    out_full = pl.BlockSpec((nb, gw), lambda b, pt: (0, 0))
    return pl.pallas_call(
        functools.partial(_decode_kernel, g_pages=g_pages, layer=layer),
        out_shape=[jax.ShapeDtypeStruct((nb, gw), F32), jax.ShapeDtypeStruct((nb, gw), F32)],
        grid_spec=pltpu.PrefetchScalarGridSpec(
            num_scalar_prefetch=1,
            grid=(nb,),
            in_specs=in_specs,
            out_specs=[out_full, out_full],
            scratch_shapes=[pltpu.VMEM((DECODE_RING, g_pages, N_HEADS, HEAD_DIM, page), F32),
                            pltpu.VMEM((DECODE_RING, g_pages, N_HEADS, HEAD_DIM, page), F32),
                            pltpu.VMEM((DECODE_RING, g_pages, page, MLA_KV_RANK), F32),
                            pltpu.VMEM((DECODE_RING, g_pages, MLA_ROPE, page), F32),
                            pltpu.SemaphoreType.DMA((4, DECODE_RING)),
                            pltpu.VMEM((N_HEADS, nb, MLA_KV_RANK), F32),
                            pltpu.VMEM((N_HEADS, nb, MLA_KV_RANK), F32)]),
        compiler_params=pltpu.CompilerParams(
            dimension_semantics=("arbitrary",), vmem_limit_bytes=VMEM_LIMIT_BYTES),
        name="decode_attn",
    )(page_table, bias, *vm, k_pool, v_pool, ckv_pool, kr_pool)


def _swap_halves(w):
    half = w.shape[-1] // 2
    return jnp.concatenate([w[..., half:], w[..., :half]], axis=-1)


def _layer_weights(p, l):
    gw = GROUP_WIDTH
    w_in = p['w_in'][l]
    sizes = (gw, gw, gw, N_HEADS, gw, gw, gw, gw, gw, GLA_LOWRANK, MLA_Q_RANK, MLA_KV_RANK, MLA_ROPE)
    offs = np.concatenate([[0], np.cumsum(sizes)])
    fq, fk, fv, ff, su, gq, gk, gv, gg, glr, mq, mkv, mkr = [
        w_in[:, int(offs[i]):int(offs[i + 1])] for i in range(len(sizes))]
    d = w_in.shape[0]
    small = jnp.concatenate([mkr, _swap_halves(mkr), glr, ff,
                             jnp.zeros((d, LANES - 2 * MLA_ROPE - GLA_LOWRANK - N_HEADS), F32)], axis=1)
    wcat = jnp.concatenate([fq, fk, fv, su, gq, gk, gv, gg, mkv, small, mq,
                            jnp.zeros((d, N_ZCOLS - C_MQ - MLA_Q_RANK), F32)], axis=1).astype(BF16)

    w_uq = p['w_mla_uq'][l].reshape(MLA_Q_RANK, N_HEADS, HEAD_DIM + MLA_ROPE)
    nope, ropew = w_uq[..., :HEAD_DIM], w_uq[..., HEAD_DIM:]
    zpad = jnp.zeros((MLA_Q_RANK, N_HEADS, MLA_SLOT - HEAD_DIM - MLA_ROPE), F32)
    w_uq_a = jnp.concatenate([nope, ropew, zpad], axis=-1).reshape(MLA_Q_RANK, N_HEADS * MLA_SLOT)
    w_uq_b = jnp.concatenate([jnp.zeros_like(nope), _swap_halves(ropew), zpad],
                             axis=-1).reshape(MLA_Q_RANK, N_HEADS * MLA_SLOT)
    w_uk = p['w_mla_uk'][l]
    w_uk_slots = jnp.concatenate(
        [w_uk, jnp.zeros((MLA_KV_RANK, N_HEADS, MLA_SLOT - HEAD_DIM), F32)],
        axis=-1).reshape(MLA_KV_RANK, N_HEADS * MLA_SLOT)
    p_kr = jnp.concatenate([jnp.zeros((MLA_ROPE, HEAD_DIM), F32), jnp.eye(MLA_ROPE, dtype=F32),
                            jnp.zeros((MLA_ROPE, MLA_SLOT - HEAD_DIM - MLA_ROPE), F32)], axis=1)
    p_kr = jnp.tile(p_kr, (1, N_HEADS))

    abr, abi, bin_re, bin_im = _s5_discretize(p['s5_a_re'][l], p['s5_a_im'][l], p['s5_log_dt'][l],
                                              p['s5_b_re'][l], p['s5_b_im'][l])
    row = lambda a: a.reshape(1, -1).astype(F32)
    return dict(
        wcat=wcat, norm_mix_pre=row(p['norm_mix_pre'][l]), b_fox_f=row(p['b_fox_f'][l]),
        w_gla_gk=p['w_gla_gk'][l].astype(BF16), b_gla_gk=row(p['b_gla_gk'][l]),
        mla_q_norm=row(p['mla_q_norm'][l]), w_uq_a=w_uq_a.astype(BF16), w_uq_b=w_uq_b.astype(BF16),
        mla_kv_norm=row(p['mla_kv_norm'][l]), w_uk_slots=w_uk_slots.astype(BF16),
        w_uv=p['w_mla_uv'][l].reshape(MLA_KV_RANK, gw).astype(BF16), p_kr=p_kr.astype(BF16),
        w_uk_h=jnp.transpose(w_uk, (1, 0, 2)).astype(BF16),
        w_uv_h=jnp.transpose(p['w_mla_uv'][l], (1, 0, 2)).astype(BF16),
        s5_abar_re=abr, s5_abar_im=abi, s5_bin_re=bin_re, s5_bin_im=bin_im,
        s5_cout_re=_s5_out_blockdiag(p['s5_c_re'][l]), s5_cout_im=_s5_out_blockdiag(p['s5_c_im'][l]),
        s5_d=row(p['s5_d'][l]), w_s5_glu=p['w_s5_glu'][l].astype(BF16), b_s5_glu=row(p['b_s5_glu'][l]),
        gla_norm=row(p['gla_norm'][l]),
        w_out=p['w_out'][l].astype(BF16), norm_mix_post=row(p['norm_mix_post'][l]),
        norm_ffn_pre=row(p['norm_ffn_pre'][l]), norm_ffn_post=row(p['norm_ffn_post'][l]),
        w_ffn_gate=p['w_ffn_gate'][l].astype(BF16), w_ffn_up=p['w_ffn_up'][l].astype(BF16),
        w_ffn_down=p['w_ffn_down'][l].astype(BF16))


def _rope_tables(pos):
    half = MLA_ROPE // 2
    inv = ROPE_THETA ** (-jnp.arange(half, dtype=F32) * 2.0 / MLA_ROPE)
    ang = pos.astype(F32)[:, None] * inv[None, :]
    cos, sin = jnp.cos(ang), jnp.sin(ang)
    n = pos.shape[0]
    c32 = jnp.concatenate([cos, cos], axis=-1)
    s32 = jnp.concatenate([-sin, sin], axis=-1)
    ones = jnp.ones((n, HEAD_DIM), F32)
    zeros = jnp.zeros((n, HEAD_DIM), F32)
    zpad = jnp.zeros((n, MLA_SLOT - HEAD_DIM - MLA_ROPE), F32)
    tqc = jnp.tile(jnp.concatenate([ones, c32, zpad], axis=-1), (1, N_HEADS))
    tqs = jnp.tile(jnp.concatenate([zeros, s32, zpad], axis=-1), (1, N_HEADS))
    return tqc, tqs, jnp.concatenate([c32, s32], axis=-1)


def kernel(x_prompt, x_sample, cache_fox_k, cache_fox_v, cache_fox_logf, cache_mla_ckv, cache_mla_krope, state_s5_re, state_s5_im, state_gla, page_table, meta_tokens, norm_mix_pre, norm_mix_post, norm_ffn_pre, norm_ffn_post, w_in, b_fox_f, s5_a_re, s5_a_im, s5_log_dt, s5_b_re, s5_b_im, s5_c_re, s5_c_im, s5_d, w_s5_glu, b_s5_glu, w_gla_gk, b_gla_gk, gla_norm, mla_q_norm, w_mla_uq, mla_kv_norm, w_mla_uk, w_mla_uv, w_out, w_ffn_gate, w_ffn_up, w_ffn_down):
    params = dict(
        norm_mix_pre=norm_mix_pre, norm_mix_post=norm_mix_post, norm_ffn_pre=norm_ffn_pre,
        norm_ffn_post=norm_ffn_post, w_in=w_in, b_fox_f=b_fox_f, s5_a_re=s5_a_re, s5_a_im=s5_a_im,
        s5_log_dt=s5_log_dt, s5_b_re=s5_b_re, s5_b_im=s5_b_im, s5_c_re=s5_c_re, s5_c_im=s5_c_im,
        s5_d=s5_d, w_s5_glu=w_s5_glu, b_s5_glu=b_s5_glu, w_gla_gk=w_gla_gk, b_gla_gk=b_gla_gk,
        gla_norm=gla_norm, mla_q_norm=mla_q_norm, w_mla_uq=w_mla_uq, mla_kv_norm=mla_kv_norm,
        w_mla_uk=w_mla_uk, w_mla_uv=w_mla_uv, w_out=w_out, w_ffn_gate=w_ffn_gate,
        w_ffn_up=w_ffn_up, w_ffn_down=w_ffn_down)
    depth = w_in.shape[0]
    nbp, seq, d = x_prompt.shape
    nbs, dec_seq, _ = x_sample.shape
    assert dec_seq == 1, "the sample path handles one new token per sequence"
    t = seq + N_META
    n_pages, page = page_table.shape[1], cache_fox_k.shape[2]
    past_len = n_pages * page
    gw = GROUP_WIDTH

    hp = jnp.concatenate([jnp.broadcast_to(meta_tokens.astype(x_prompt.dtype)[None], (nbp, N_META, d)),
                          x_prompt], axis=1)
    hs = x_sample.reshape(1, nbs, d)
    tabs_p = _rope_tables(jnp.arange(t))
    tabs_s = _rope_tables(jnp.full((nbs,), past_len, jnp.int32))
    k_pool_t = jnp.transpose(cache_fox_k, (0, 1, 3, 4, 2))
    v_pool_t = jnp.transpose(cache_fox_v, (0, 1, 3, 4, 2))
    kr_pool_t = jnp.transpose(cache_mla_krope, (0, 1, 3, 2))

    rows_p, rows_s = [], []
    for l in range(depth):
        lw = _layer_weights(params, l)

        (fq, fk, fv, fkb, fvb, logf, c, su_tm, gla_in, qcat, kcat, vmla, ckv, kr) = _proj_call(
            hp, lw, tabs_p, cumsum=True)
        fox_o = _attn_call(fq, fkb, fvb, c)
        mla_o = _attn_call(qcat, kcat, vmla)
        zeros_state = jnp.zeros((nbp, S5_LANES), F32)
        s5_o, s5_re, s5_im = _s5_call(su_tm.reshape(t * nbp, gw), zeros_state, zeros_state, lw, nbp)
        gla_o, gla_st = _gla_call(gla_in, jnp.zeros((nbp, gw, HEAD_DIM), F32), lw['gla_norm'])
        hp = _finish_call(hp, fox_o, s5_o.reshape(t, nbp * gw), gla_o, mla_o, lw)
        rows_p.append((fk.reshape(nbp, t, N_HEADS, HEAD_DIM), fv.reshape(nbp, t, N_HEADS, HEAD_DIM),
                       logf, ckv, kr, s5_re.reshape(nbp, S5_GROUPS, S5_STATE),
                       s5_im.reshape(nbp, S5_GROUPS, S5_STATE),
                       jnp.swapaxes(gla_st.reshape(nbp, N_HEADS, HEAD_DIM, HEAD_DIM), 2, 3)))

        (fq, fk, fv, fkb, fvb, logf, c, su_tm, gla_in, qcat, kcat, vmla, ckv, kr) = _proj_call(
            hs, lw, tabs_s, cumsum=False)
        logf_t = jnp.pad(jnp.swapaxes(cache_fox_logf[l].astype(F32), 1, 2),
                         ((0, 0), (0, SUBLANES - N_HEADS), (0, 0)))
        bias = _fox_bias_call(page_table, logf_t)
        cnew8 = jnp.concatenate([logf[0], jnp.zeros((nbs, SUBLANES - N_HEADS), F32)], axis=1)
        fox_o, mla_o = _decode_call(
            page_table, bias, fq[0].astype(F32), qcat[0].astype(F32), cnew8, fk[0], fv[0], ckv[0], kr[0],
            lw['w_uk_h'], lw['w_uv_h'], k_pool_t, v_pool_t, cache_mla_ckv, kr_pool_t, l)
        s5_o, s5_re, s5_im = _s5_call(su_tm, state_s5_re[l].reshape(nbs, S5_LANES),
                                      state_s5_im[l].reshape(nbs, S5_LANES), lw, nbs)
        gla_pad = jnp.concatenate([jnp.zeros((nbs, SUBLANES - 1, 5 * gw), F32),
                                   gla_in.reshape(nbs, 1, 5 * gw)], axis=1)
        s0t = jnp.swapaxes(state_gla[l].astype(F32), 2, 3).reshape(nbs, gw, HEAD_DIM)
        gla_o, gla_st = _gla_call(gla_pad, s0t, lw['gla_norm'])
        gla_o = gla_o[:, SUBLANES - 1, :].reshape(1, nbs, gw)
        hs = _finish_call(hs, fox_o.reshape(1, nbs, gw), s5_o, gla_o, mla_o.reshape(1, nbs, gw), lw)
        rows_s.append((fk.reshape(nbs, 1, N_HEADS, HEAD_DIM), fv.reshape(nbs, 1, N_HEADS, HEAD_DIM),
                       logf.reshape(nbs, 1, N_HEADS), ckv.reshape(nbs, 1, MLA_KV_RANK),
                       kr.reshape(nbs, 1, MLA_ROPE), s5_re.reshape(nbs, S5_GROUPS, S5_STATE),
                       s5_im.reshape(nbs, S5_GROUPS, S5_STATE),
                       jnp.swapaxes(gla_st.reshape(nbs, N_HEADS, HEAD_DIM, HEAD_DIM), 2, 3)))

    fk_p, fv_p, flf_p, ckv_p, kr_p, s5re_p, s5im_p, gla_p = [jnp.stack(r) for r in zip(*rows_p)]
    fk_s, fv_s, flf_s, ckv_s, kr_s, s5re_s, s5im_s, gla_s = [jnp.stack(r) for r in zip(*rows_s)]
    y_prompt = hp[:, N_META:]
    y_sample = hs.reshape(nbs, 1, d)
    return (y_prompt, y_sample, fk_p, fk_s, fv_p, fv_s, flf_p, flf_s, ckv_p, ckv_s, kr_p, kr_s,
            s5re_p, s5re_s, s5im_p, s5im_s, gla_p, gla_s)
```

```python
import functools
import math

import jax
import jax.numpy as jnp
import numpy as np
from jax import lax
from jax.experimental import pallas as pl
from jax.experimental.pallas import tpu as pltpu

F32 = jnp.float32
BF16 = jnp.bfloat16
HIGHEST = lax.Precision.HIGHEST

N_META = 16
HEAD_DIM = 64
NORM_EPS = 1e-6
N_HEADS = 4
GROUP_WIDTH = N_HEADS * HEAD_DIM
S5_GC = 16
S5_GROUPS = GROUP_WIDTH // S5_GC
S5_STATE = 64
S5_LANES = S5_GROUPS * S5_STATE
GLA_LOWRANK = 16
GLA_GATE_NORM = 16.0
MLA_ROPE = HEAD_DIM // 2
MLA_Q_RANK = 3 * GROUP_WIDTH // 4
MLA_KV_RANK = GROUP_WIDTH // 2
FOX_SCALE = HEAD_DIM ** -0.5
GLA_SCALE = HEAD_DIM ** -0.5
MLA_SCALE = (HEAD_DIM + MLA_ROPE) ** -0.5
ROPE_THETA = 10000.0
LOG2E = math.log2(math.e)
MLA_SLOT = 128

SUBLANES = 8
LANES = 128
VMEM_LIMIT_BYTES = 56 * 1024 * 1024

PROJ_TILE = 688
ATTN_TILE = 384
S5_CHUNK = 48
GLA_CHUNK = 48
GLA_BATCH_BLOCK = 8
FFN_TILE = 344
DECODE_PAGES_PER_STEP = 16
DECODE_RING = 3
BIAS_RING = 3

C_FOX = 0
C_S5 = 768
C_GLA = 1024
C_MKV = 2048
C_SMALL = 2176
C_MQ = 2304
N_ZCOLS = 2560


def _pick_tile(n, target):
    best = None
    for t in range(SUBLANES, min(n, target) + 1, SUBLANES):
        if n % t == 0:
            best = t
    return best if best is not None else n


def _rms(x, w):
    return x * lax.rsqrt(jnp.mean(x * x, axis=-1, keepdims=True) + NORM_EPS) * w


def _log_sigmoid(x):
    return jnp.minimum(x, 0.0) - jnp.log1p(jnp.exp(-jnp.abs(x)))


def _sigmoid(x):
    return 1.0 / (1.0 + jnp.exp(-x))


def _dot(a, b):
    return jnp.dot(a, b, preferred_element_type=F32)


def _dot_nt(a, b):
    return lax.dot_general(a, b, (((1,), (1,)), ((), ())), preferred_element_type=F32)


def _dot_tn(a, b):
    return lax.dot_general(a, b, (((0,), (0,)), ((), ())), preferred_element_type=F32)


def _bf16_pieces(x):
    hi = x.astype(BF16)
    r1 = x - hi.astype(F32)
    mid = r1.astype(BF16)
    lo = (r1 - mid.astype(F32)).astype(BF16)
    return hi, mid, lo


def _dot_exact_lhs(a01, x):
    hi, mid, lo = _bf16_pieces(x)
    return _dot(a01, hi) + _dot(a01, mid) + _dot(a01, lo)


def _const_spec(shape):
    nd = len(shape)
    return pl.BlockSpec(shape, lambda *_: (0,) * nd)


def _proj_kernel(x_ref, nw_ref, w_ref, bff_ref, wgk_ref, bgk_ref, qnw_ref, wuqa_ref, wuqb_ref,
                 kvnw_ref, wuk_ref, wuv_ref, pkr_ref, tqc_ref, tqs_ref, tkr_ref,
                 fq_ref, fk_ref, fv_ref, fkb_ref, fvb_ref, logf_ref, c_ref, su_ref, gla_ref,
                 qcat_ref, kcat_ref, vmla_ref, ckv_ref, kr_ref, carry_ref, *, cumsum):
    tm = x_ref.shape[0]
    hn = _rms(x_ref[...], nw_ref[...]).astype(BF16)

    zf = _dot(hn, w_ref[:, C_FOX:C_FOX + 3 * GROUP_WIDTH])
    fq_ref[...] = (zf[:, :GROUP_WIDTH] * FOX_SCALE).astype(BF16)
    fk = zf[:, GROUP_WIDTH:2 * GROUP_WIDTH]
    fv = zf[:, 2 * GROUP_WIDTH:]
    fk_ref[...] = fk
    fv_ref[...] = fv
    fkb_ref[...] = fk.astype(BF16)
    fvb_ref[...] = fv.astype(BF16)

    su_ref[...] = _dot(hn, w_ref[:, C_S5:C_S5 + GROUP_WIDTH])

    zs = _dot(hn, w_ref[:, C_SMALL:C_SMALL + LANES])
    mkr = zs[:, 0:MLA_ROPE]
    mkr_sw = zs[:, MLA_ROPE:2 * MLA_ROPE]
    glr = zs[:, 2 * MLA_ROPE:2 * MLA_ROPE + GLA_LOWRANK]
    ff = zs[:, 2 * MLA_ROPE + GLA_LOWRANK:2 * MLA_ROPE + GLA_LOWRANK + N_HEADS]

    logf = _log_sigmoid(ff + bff_ref[...])
    logf_ref[...] = logf
    if cumsum:
        ti = pl.program_id(1)

        @pl.when(ti == 0)
        def _():
            carry_ref[...] = jnp.zeros_like(carry_ref)

        row = lax.broadcasted_iota(jnp.int32, (tm, tm), 0)
        col = lax.broadcasted_iota(jnp.int32, (tm, tm), 1)
        tri = jnp.where(row >= col, 1.0, 0.0).astype(BF16)
        c = _dot_exact_lhs(tri, logf) + carry_ref[...]
        c_ref[...] = c
        carry_ref[...] = c[tm - 1:tm, :]
    else:
        c_ref[...] = logf

    tkr = tkr_ref[...]
    kr = mkr * tkr[:, :MLA_ROPE] + mkr_sw * tkr[:, MLA_ROPE:]
    kr_ref[...] = kr

    glog = _log_sigmoid(_dot(glr.astype(BF16), wgk_ref[...]) + bgk_ref[...]) * (1.0 / GLA_GATE_NORM)
    zg = _dot(hn, w_ref[:, C_GLA:C_GLA + 4 * GROUP_WIDTH])
    gla_ref[:, 0:GROUP_WIDTH] = zg[:, 0:GROUP_WIDTH] * GLA_SCALE
    gla_ref[:, GROUP_WIDTH:3 * GROUP_WIDTH] = zg[:, GROUP_WIDTH:3 * GROUP_WIDTH]
    gla_ref[:, 3 * GROUP_WIDTH:4 * GROUP_WIDTH] = glog
    gla_ref[:, 4 * GROUP_WIDTH:5 * GROUP_WIDTH] = zg[:, 3 * GROUP_WIDTH:]

    mkv = _dot(hn, w_ref[:, C_MKV:C_MKV + MLA_KV_RANK])
    ckv = _rms(mkv, kvnw_ref[...])
    ckv_ref[...] = ckv
    ckv_b = ckv.astype(BF16)
    kcat = _dot(ckv_b, wuk_ref[...]) + _dot(kr.astype(BF16), pkr_ref[...])
    kcat_ref[...] = kcat.astype(BF16)
    vmla_ref[...] = _dot(ckv_b, wuv_ref[...]).astype(BF16)

    mq = _dot(hn, w_ref[:, C_MQ:C_MQ + MLA_Q_RANK])
    qln = _rms(mq, qnw_ref[...]).astype(BF16)
    qcat = _dot(qln, wuqa_ref[...]) * tqc_ref[...] + _dot(qln, wuqb_ref[...]) * tqs_ref[...]
    qcat_ref[...] = (qcat * MLA_SCALE).astype(BF16)


def _proj_call(x3, lw, tabs, *, cumsum):
    nb, t, d = x3.shape
    tm = _pick_tile(t, PROJ_TILE)
    nt = t // tm
    tqc, tqs, tkr = tabs
    gw = GROUP_WIDTH
    qs = N_HEADS * MLA_SLOT

    def tok(width):
        return pl.BlockSpec((None, tm, width), lambda b, i: (b, i, 0))

    def tab(width):
        return pl.BlockSpec((tm, width), lambda b, i: (i, 0))

    consts = [lw['norm_mix_pre'], lw['wcat'], lw['b_fox_f'], lw['w_gla_gk'], lw['b_gla_gk'],
              lw['mla_q_norm'], lw['w_uq_a'], lw['w_uq_b'], lw['mla_kv_norm'], lw['w_uk_slots'],
              lw['w_uv'], lw['p_kr']]
    in_specs = [tok(d)] + [_const_spec(c.shape) for c in consts] + [tab(qs), tab(qs), tab(2 * MLA_ROPE)]
    out_shape = [
        jax.ShapeDtypeStruct((nb, t, gw), BF16),
        jax.ShapeDtypeStruct((nb, t, gw), F32),
        jax.ShapeDtypeStruct((nb, t, gw), F32),
        jax.ShapeDtypeStruct((nb, t, gw), BF16),
        jax.ShapeDtypeStruct((nb, t, gw), BF16),
        jax.ShapeDtypeStruct((nb, t, N_HEADS), F32),
        jax.ShapeDtypeStruct((nb, t, N_HEADS), F32),
        jax.ShapeDtypeStruct((t, nb * gw), F32),
        jax.ShapeDtypeStruct((nb, t, 5 * gw), F32),
        jax.ShapeDtypeStruct((nb, t, qs), BF16),
        jax.ShapeDtypeStruct((nb, t, qs), BF16),
        jax.ShapeDtypeStruct((nb, t, gw), BF16),
        jax.ShapeDtypeStruct((nb, t, MLA_KV_RANK), F32),
        jax.ShapeDtypeStruct((nb, t, MLA_ROPE), F32),
    ]
    out_specs = [tok(gw), tok(gw), tok(gw), tok(gw), tok(gw), tok(N_HEADS), tok(N_HEADS),
                 pl.BlockSpec((tm, gw), lambda b, i: (i, b)),
                 tok(5 * gw), tok(qs), tok(qs), tok(gw), tok(MLA_KV_RANK), tok(MLA_ROPE)]
    return pl.pallas_call(
        functools.partial(_proj_kernel, cumsum=cumsum),
        out_shape=out_shape,
        grid=(nb, nt),
        in_specs=in_specs,
        out_specs=out_specs,
        scratch_shapes=[pltpu.VMEM((1, N_HEADS), F32)],
        compiler_params=pltpu.CompilerParams(
            dimension_semantics=("arbitrary", "arbitrary"), vmem_limit_bytes=VMEM_LIMIT_BYTES),
        name="proj",
    )(x3, *consts, tqc, tqs, tkr)


def _attn_kernel(*refs, dk, has_bias):
    if has_bias:
        q_ref, k_ref, v_ref, c_ref, o_ref, qp_ref, kp_ref, vp_ref, s_ref, cp_ref = refs
    else:
        q_ref, k_ref, v_ref, o_ref, qp_ref, kp_ref, vp_ref, s_ref = refs
    t = k_ref.shape[0]
    tq = o_ref.shape[0]
    tp = kp_ref.shape[0]
    qi = pl.program_id(1)

    @pl.when(qi == 0)
    def _():
        pairs = [(q_ref, qp_ref, None), (k_ref, kp_ref, None), (v_ref, vp_ref, None)]
        if has_bias:
            pairs.append((c_ref, cp_ref, LOG2E))
        for src, dst, scale in pairs:
            dst[0:t, :] = src[...] if scale is None else src[...] * scale
            dst[t:tp, :] = jnp.zeros((tp - t, dst.shape[1]), dst.dtype)

    q0 = pl.multiple_of(qi * tq, tq)
    causal = (lax.broadcasted_iota(jnp.int32, (tq, tq), 0)
              <= lax.broadcasted_iota(jnp.int32, (tq, tq), 1))

    qs = [qp_ref[pl.ds(q0, tq), h * dk:(h + 1) * dk] for h in range(N_HEADS)]

    groups = tq // SUBLANES

    def fold(x):
        return x.reshape(groups, SUBLANES, tq)

    def score_tile(h, k0, kj, masked, mx):
        s = _dot_nt(kp_ref[pl.ds(k0, tq), h * dk:(h + 1) * dk], qs[h]) * LOG2E
        if has_bias:
            s = s - cp_ref[pl.ds(k0, tq), h:h + 1]
        if masked:
            s = jnp.where(causal, s, -jnp.inf)
        s_ref[h, kj] = s
        return jnp.maximum(mx, jnp.max(fold(s), axis=0))

    def pass1(kj, mxs):
        k0 = pl.multiple_of(kj * tq, tq)
        return tuple(score_tile(h, k0, kj, False, mxs[h]) for h in range(N_HEADS))

    mxs = lax.fori_loop(0, qi, pass1,
                        tuple(jnp.full((SUBLANES, tq), -jnp.inf, F32) for _ in range(N_HEADS)))
    ms = [jnp.max(score_tile(h, q0, qi, True, mxs[h]), axis=0, keepdims=True) for h in range(N_HEADS)]

    def pass2(kj, carry):
        k0 = pl.multiple_of(kj * tq, tq)
        new = []
        for h in range(N_HEADS):
            lsum, acc = carry[h]
            p = jnp.exp2(s_ref[h, kj] - ms[h])
            v = vp_ref[pl.ds(k0, tq), h * HEAD_DIM:(h + 1) * HEAD_DIM]
            new.append((lsum + jnp.sum(fold(p), axis=0),
                        acc + _dot_tn(v, p.astype(BF16))))
        return tuple(new)

    carry = lax.fori_loop(0, qi + 1, pass2,
                          tuple((jnp.zeros((SUBLANES, tq), F32), jnp.zeros((HEAD_DIM, tq), F32))
                                for _ in range(N_HEADS)))
    outs = [acc / jnp.sum(lsum, axis=0, keepdims=True) for (lsum, acc) in carry]
    for pair in range(N_HEADS // 2):
        o_ref[:, pair * 2 * HEAD_DIM:(pair + 1) * 2 * HEAD_DIM] = jnp.concatenate(
            outs[2 * pair:2 * pair + 2], axis=0).T


def _attn_call(q, k, v, c=None):
    nb, t, qw = q.shape
    dk = qw // N_HEADS
    tq = ATTN_TILE
    nq = pl.cdiv(t, tq)
    tp = nq * tq
    has_bias = c is not None

    def seq(width):
        return pl.BlockSpec((None, t, width), lambda b, i: (b, 0, 0))

    in_specs = [seq(qw), seq(qw), seq(GROUP_WIDTH)]
    args = [q, k, v]
    scratch = [pltpu.VMEM((tp, qw), BF16), pltpu.VMEM((tp, qw), BF16), pltpu.VMEM((tp, GROUP_WIDTH), BF16),
               pltpu.VMEM((N_HEADS, nq, tq, tq), F32)]
    if has_bias:
        in_specs.append(seq(N_HEADS))
        args.append(c)
        scratch.append(pltpu.VMEM((tp, N_HEADS), F32))
    return pl.pallas_call(
        functools.partial(_attn_kernel, dk=dk, has_bias=has_bias),
        out_shape=jax.ShapeDtypeStruct((nb, tp, GROUP_WIDTH), F32),
        grid=(nb, nq),
        in_specs=in_specs,
        out_specs=pl.BlockSpec((None, tq, GROUP_WIDTH), lambda b, i: (b, i, 0)),
        scratch_shapes=scratch,
        compiler_params=pltpu.CompilerParams(
            dimension_semantics=("arbitrary", "arbitrary"), vmem_limit_bytes=VMEM_LIMIT_BYTES),
        name="fox_attn" if has_bias else "mla_attn",
    )(*args)


def _s5_disc_kernel(are_ref, aim_ref, ldt_ref, bre_ref, bim_ref,
                    abr_ref, abi_ref, bbr_ref, bbi_ref):
    dt = jnp.exp(ldt_ref[...])
    ar = are_ref[...]
    ai = aim_ref[...]
    mag = jnp.exp(ar * dt)
    abr = mag * jnp.cos(ai * dt)
    abi = mag * jnp.sin(ai * dt)
    nr = abr - 1.0
    ni = abi
    den = ar * ar + ai * ai
    fr = (nr * ar + ni * ai) / den
    fi = (ni * ar - nr * ai) / den
    abr_ref[...] = abr
    abi_ref[...] = abi
    for c in range(S5_GC):
        br = bre_ref[c]
        bi = bim_ref[c]
        bbr_ref[c] = fr * br - fi * bi
        bbi_ref[c] = fr * bi + fi * br


def _s5_discretize(a_re, a_im, log_dt, b_re, b_im):
    g, p = a_re.shape
    bt = lambda b: jnp.transpose(b, (2, 0, 1))
    outs = pl.pallas_call(
        _s5_disc_kernel,
        out_shape=[jax.ShapeDtypeStruct((g, p), F32), jax.ShapeDtypeStruct((g, p), F32),
                   jax.ShapeDtypeStruct((S5_GC, g, p), F32), jax.ShapeDtypeStruct((S5_GC, g, p), F32)],
        name="s5_discretize",
    )(a_re, a_im, log_dt.reshape(g, 1), bt(b_re), bt(b_im))
    abr, abi, bbr, bbi = outs
    eye = jnp.eye(g, dtype=F32)

    def in_blockdiag(bb):
        m = jnp.transpose(bb, (1, 0, 2))[:, :, None, :] * eye[:, None, :, None]
        return m.reshape(g * S5_GC, g * p).astype(BF16)

    return abr.reshape(1, g * p), abi.reshape(1, g * p), in_blockdiag(bbr), in_blockdiag(bbi)


def _s5_out_blockdiag(c):
    g = c.shape[0]
    eye = jnp.eye(g, dtype=F32)
    m = jnp.transpose(c, (0, 2, 1))[:, :, None, :] * eye[:, None, :, None]
    return m.reshape(g * c.shape[2], g * c.shape[1]).astype(BF16)


def _gelu_tanh(x):
    return 0.5 * x * (1.0 + jnp.tanh(math.sqrt(2.0 / math.pi) * (x + 0.044715 * (x * x * x))))


def _s5_kernel(u_ref, h0r_ref, h0i_ref, ar_ref, ai_ref, bre_ref, bim_ref, cre_ref, cim_ref,
               d_ref, wglu_ref, bglu_ref, o_ref, hr_ref, hi_ref, sr_ref, si_ref, *, nb):
    i = pl.program_id(0)
    tc = u_ref.shape[0] // nb

    @pl.when(i == 0)
    def _():
        hr_ref[...] = h0r_ref[...]
        hi_ref[...] = h0i_ref[...]

    u = u_ref[...]
    ub = u.astype(BF16)
    sr_ref[...] = _dot(ub, bre_ref[...])
    si_ref[...] = _dot(ub, bim_ref[...])
    ar = ar_ref[...]
    ai = ai_ref[...]

    def step(t, carry):
        hr, hi = carry
        r0 = pl.multiple_of(t * nb, nb)
        nhr = ar * hr - ai * hi + sr_ref[pl.ds(r0, nb), :]
        nhi = ar * hi + ai * hr + si_ref[pl.ds(r0, nb), :]
        sr_ref[pl.ds(r0, nb), :] = nhr
        si_ref[pl.ds(r0, nb), :] = nhi
        return nhr, nhi

    hr, hi = lax.fori_loop(0, tc, step, (hr_ref[...], hi_ref[...]))
    hr_ref[...] = hr
    hi_ref[...] = hi

    y = (_dot(sr_ref[...].astype(BF16), cre_ref[...]) - _dot(si_ref[...].astype(BF16), cim_ref[...])
         + d_ref[...] * u)
    g = _gelu_tanh(y)
    o_ref[...] = g * _sigmoid(_dot(g.astype(BF16), wglu_ref[...]) + bglu_ref[...])


def _s5_call(u_tm, h0r, h0i, lw, nb):
    rows = u_tm.shape[0]
    t = rows // nb
    tc = _pick_tile(t, S5_CHUNK) if t >= SUBLANES else t
    r = tc * nb
    consts = [lw['s5_abar_re'], lw['s5_abar_im'], lw['s5_bin_re'], lw['s5_bin_im'],
              lw['s5_cout_re'], lw['s5_cout_im'], lw['s5_d'], lw['w_s5_glu'], lw['b_s5_glu']]
    state_spec = _const_spec((nb, S5_LANES))
    return pl.pallas_call(
        functools.partial(_s5_kernel, nb=nb),
        out_shape=[jax.ShapeDtypeStruct((rows, GROUP_WIDTH), F32),
                   jax.ShapeDtypeStruct((nb, S5_LANES), F32),
                   jax.ShapeDtypeStruct((nb, S5_LANES), F32)],
        grid=(t // tc,),
        in_specs=[pl.BlockSpec((r, GROUP_WIDTH), lambda i: (i, 0)), state_spec, state_spec]
                 + [_const_spec(c.shape) for c in consts],
        out_specs=[pl.BlockSpec((r, GROUP_WIDTH), lambda i: (i, 0)), state_spec, state_spec],
        scratch_shapes=[pltpu.VMEM((r, S5_LANES), F32), pltpu.VMEM((r, S5_LANES), F32)],
        compiler_params=pltpu.CompilerParams(
            dimension_semantics=("arbitrary",), vmem_limit_bytes=VMEM_LIMIT_BYTES),
        name="s5_scan",
    )(u_tm, h0r, h0i, *consts)


def _gla_kernel(x_ref, s0_ref, nw_ref, o_ref, st_ref):
    ci = pl.program_id(1)
    bb, c = x_ref.shape[0], x_ref.shape[1]
    gw = GROUP_WIDTH

    @pl.when(ci == 0)
    def _():
        st_ref[...] = s0_ref[...]

    row = lax.broadcasted_iota(jnp.int32, (c, c), 0)
    col = lax.broadcasted_iota(jnp.int32, (c, c), 1)
    causal = row >= col
    tri = jnp.where(causal, 1.0, 0.0).astype(BF16)

    for bi in range(bb):
        q = x_ref[bi, :, 0:gw]
        k = x_ref[bi, :, gw:2 * gw]
        v = x_ref[bi, :, 2 * gw:3 * gw]
        g = x_ref[bi, :, 3 * gw:4 * gw]
        gate = x_ref[bi, :, 4 * gw:5 * gw]
        bc = _dot_exact_lhs(tri, g)
        b_last = bc[c - 1:c, :]
        b_mid = bc[c // 2:c // 2 + 1, :]
        q_in = (q * jnp.exp(bc - b_mid)).astype(BF16)
        k_in = (k * jnp.exp(b_mid - bc)).astype(BF16)
        q_st = (q * jnp.exp(bc)).astype(BF16)
        k_st = (k * jnp.exp(b_last - bc)).astype(BF16)
        vb = v.astype(BF16)
        decay = jnp.exp(b_last)
        for h in range(N_HEADS):
            sl = slice(h * HEAD_DIM, (h + 1) * HEAD_DIM)
            att = jnp.where(causal, _dot_nt(q_in[:, sl], k_in[:, sl]), 0.0)
            st = st_ref[bi, sl, :]
            o = _dot_nt(q_st[:, sl], st.astype(BF16)) + _dot(att.astype(BF16), vb[:, sl])
            st_ref[bi, sl, :] = st * decay[:, sl] + _dot_tn(vb[:, sl], k_st[:, sl])
            o = _rms(o, nw_ref[...])
            gt = gate[:, sl]
            o_ref[bi, :, sl] = o * (gt * _sigmoid(gt))


def _gla_call(gla_in, s0t, norm_w):
    nb, t, w = gla_in.shape
    c = _pick_tile(t, GLA_CHUNK)
    bb = math.gcd(nb, GLA_BATCH_BLOCK)
    return pl.pallas_call(
        _gla_kernel,
        out_shape=[jax.ShapeDtypeStruct((nb, t, GROUP_WIDTH), F32),
                   jax.ShapeDtypeStruct((nb, GROUP_WIDTH, HEAD_DIM), F32)],
        grid=(nb // bb, t // c),
        in_specs=[pl.BlockSpec((bb, c, w), lambda b, i: (b, i, 0)),
                  pl.BlockSpec((bb, GROUP_WIDTH, HEAD_DIM), lambda b, i: (b, 0, 0)),
                  _const_spec(norm_w.shape)],
        out_specs=[pl.BlockSpec((bb, c, GROUP_WIDTH), lambda b, i: (b, i, 0)),
                   pl.BlockSpec((bb, GROUP_WIDTH, HEAD_DIM), lambda b, i: (b, 0, 0))],
        compiler_params=pltpu.CompilerParams(
            dimension_semantics=("arbitrary", "arbitrary"), vmem_limit_bytes=VMEM_LIMIT_BYTES),
        name="gla",
    )(gla_in, s0t, norm_w)


def _finish_kernel(x_ref, fox_ref, s5_ref, gla_ref, mla_ref, wout_ref, npost_ref, npre_ref,
                   nffn_ref, wg_ref, wu_ref, wd_ref, o_ref):
    gw = GROUP_WIDTH
    mix = (_dot(fox_ref[...].astype(BF16), wout_ref[0:gw, :])
           + _dot(s5_ref[...].astype(BF16), wout_ref[gw:2 * gw, :])
           + _dot(gla_ref[...].astype(BF16), wout_ref[2 * gw:3 * gw, :])
           + _dot(mla_ref[...].astype(BF16), wout_ref[3 * gw:4 * gw, :]))
    x1 = x_ref[...] + _rms(mix, npost_ref[...])
    hf = _rms(x1, npre_ref[...]).astype(BF16)
    gate = _dot(hf, wg_ref[...])
    act = (gate * _sigmoid(gate) * _dot(hf, wu_ref[...])).astype(BF16)
    f = _dot(act, wd_ref[...])
    o_ref[...] = x1 + _rms(f, nffn_ref[...])


def _finish_call(x3, fox_o, s5_tm, gla_o, mla_o, lw):
    nb, t, d = x3.shape
    tm = _pick_tile(t, FFN_TILE)
    gw = GROUP_WIDTH

    def tok(width):
        return pl.BlockSpec((None, tm, width), lambda b, i: (b, i, 0))

    consts = [lw['w_out'], lw['norm_mix_post'], lw['norm_ffn_pre'], lw['norm_ffn_post'],
              lw['w_ffn_gate'], lw['w_ffn_up'], lw['w_ffn_down']]
    const_specs = [pl.BlockSpec(c.shape, lambda b, i: (0, 0), pipeline_mode=pl.Buffered(1))
                   for c in consts]
    return pl.pallas_call(
        _finish_kernel,
        out_shape=jax.ShapeDtypeStruct((nb, t, d), F32),
        grid=(nb, t // tm),
        in_specs=[tok(d), tok(gw), pl.BlockSpec((tm, gw), lambda b, i: (i, b)), tok(gw), tok(gw)]
                 + const_specs,
        out_specs=tok(d),
        compiler_params=pltpu.CompilerParams(
            dimension_semantics=("arbitrary", "arbitrary"), vmem_limit_bytes=VMEM_LIMIT_BYTES),
        name="mix_ffn",
    )(x3, fox_o, s5_tm, gla_o, mla_o, *consts)


def _fox_bias_kernel(pt_ref, logf_hbm, o_ref, buf_ref, sem_ref):
    b = pl.program_id(0)
    nb = pl.num_programs(0)
    n_pages = buf_ref.shape[1]
    page = buf_ref.shape[3]
    rows = n_pages * SUBLANES

    def page_copy(bb, slot, p):
        return pltpu.make_async_copy(logf_hbm.at[pt_ref[bb, p]], buf_ref.at[slot, p], sem_ref.at[slot])

    def start_all(bb, slot):
        for p in range(n_pages):
            page_copy(bb, slot, p).start()

    depth = BIAS_RING - 1

    @pl.when(b == 0)
    def _():
        for g in range(depth):
            start_all(jnp.minimum(g, nb - 1), g % BIAS_RING)

    slot = b % BIAS_RING
    start_all(jnp.minimum(b + depth, nb - 1), (b + depth) % BIAS_RING)
    for p in range(n_pages):
        page_copy(b, slot, p).wait()

    @pl.when(b == nb - 1)
    def _():
        for g in range(depth):
            for p in range(n_pages):
                page_copy(nb - 1, (nb + g) % BIAS_RING, p).wait()

    x = buf_ref[slot].reshape(rows, page)
    hi, mid, lo = _bf16_pieces(x)
    j = lax.broadcasted_iota(jnp.int32, (page, 2 * page), 0)
    s = lax.broadcasted_iota(jnp.int32, (page, 2 * page), 1)
    w = jnp.where((j > s) | (s >= page), 1.0, 0.0).astype(BF16)
    acc = (_dot(hi, w) + _dot(mid, w) + _dot(lo, w)).reshape(n_pages, SUBLANES, 2 * page)
    run = jnp.zeros((SUBLANES, page), F32)
    for p in reversed(range(n_pages)):
        o_ref[p] = acc[p, :, :page] + run
        run = run + acc[p, :, page:]


def _fox_bias_call(page_table, logf_t):
    nb, n_pages = page_table.shape
    page = logf_t.shape[2]
    rows = n_pages * SUBLANES
    return pl.pallas_call(
        _fox_bias_kernel,
        out_shape=jax.ShapeDtypeStruct((nb, n_pages, SUBLANES, page), F32),
        grid_spec=pltpu.PrefetchScalarGridSpec(
            num_scalar_prefetch=1,
            grid=(nb,),
            in_specs=[pl.BlockSpec(memory_space=pl.ANY)],
            out_specs=pl.BlockSpec((None, n_pages, SUBLANES, page), lambda b, pt: (b, 0, 0, 0)),
            scratch_shapes=[pltpu.VMEM((BIAS_RING, n_pages, SUBLANES, page), F32),
                            pltpu.SemaphoreType.DMA((BIAS_RING,))]),
        compiler_params=pltpu.CompilerParams(
            dimension_semantics=("arbitrary",), vmem_limit_bytes=VMEM_LIMIT_BYTES),
        name="fox_decode_bias",
    )(page_table, logf_t)


def _decode_kernel(pt_ref, bias_ref, fq_ref, fqt_ref, qcat_ref, cnew_ref, knew_ref, vnew_ref, ckvnew_ref,
                   krnew_ref, wuk_ref, wuv_ref, k_hbm, v_hbm, ckv_hbm, kr_hbm,
                   fox_ref, mla_ref,
                   kbuf, vbuf, cbuf, rbuf, sem_ref, qabs_ref, olat_ref, *, g_pages, layer):
    b = pl.program_id(0)
    nb = pl.num_programs(0)
    n_pages = bias_ref.shape[0]
    nc = n_pages // g_pages
    page = bias_ref.shape[2]
    gw = GROUP_WIDTH
    nb_static = fq_ref.shape[0]

    def page_copies(bb, ci, slot, j):
        pid = pt_ref[bb, ci * g_pages + j]
        return [pltpu.make_async_copy(k_hbm.at[layer, pid], kbuf.at[slot, j], sem_ref.at[0, slot]),
                pltpu.make_async_copy(v_hbm.at[layer, pid], vbuf.at[slot, j], sem_ref.at[1, slot]),
                pltpu.make_async_copy(ckv_hbm.at[layer, pid], cbuf.at[slot, j], sem_ref.at[2, slot]),
                pltpu.make_async_copy(kr_hbm.at[layer, pid], rbuf.at[slot, j], sem_ref.at[3, slot])]

    def start_page(bb, ci, slot, j):
        for cp in page_copies(bb, ci, slot, j):
            cp.start()

    def wait_chunk(bb, ci, slot):
        for j in range(g_pages):
            for cp in page_copies(bb, ci, slot, j):
                cp.wait()

    total = nb_static * nc
    depth = DECODE_RING - 1

    def chunk_of(g):
        g = jnp.minimum(g, total - 1)
        return g // nc, g % nc

    @pl.when(b == 0)
    def _():
        for g in range(depth):
            for j in range(g_pages):
                start_page(*chunk_of(g), g % DECODE_RING, j)
        for h in range(N_HEADS):
            qn = qcat_ref[:, h * MLA_SLOT:h * MLA_SLOT + HEAD_DIM].astype(BF16)
            qabs_ref[h] = _dot_nt(qn, wuk_ref[h])

    row8 = lax.broadcasted_iota(jnp.int32, (SUBLANES, gw), 0)
    lane8 = lax.broadcasted_iota(jnp.int32, (SUBLANES, gw), 1)
    head_mask = row8 == lane8 // HEAD_DIM
    qf8 = jnp.where(head_mask, jnp.broadcast_to(fq_ref[pl.ds(b, 1), :], (SUBLANES, gw)),
                    0.0).astype(BF16)
    pick = (lax.broadcasted_iota(jnp.int32, (nb_static, page), 0) == b).astype(BF16)
    qbc = _dot(fqt_ref[...].astype(BF16), pick).reshape(N_HEADS, HEAD_DIM, page)
    rowk = lax.broadcasted_iota(jnp.int32, (SUBLANES, MLA_KV_RANK), 0)
    qabs8 = jnp.zeros((SUBLANES, MLA_KV_RANK), F32)
    rowr = lax.broadcasted_iota(jnp.int32, (SUBLANES, MLA_ROPE), 0)
    qr8 = jnp.zeros((SUBLANES, MLA_ROPE), F32)
    qrow = qcat_ref[pl.ds(b, 1), :]
    for h in range(N_HEADS):
        qabs8 = jnp.where(rowk == N_HEADS + h,
                          jnp.broadcast_to(qabs_ref[h, pl.ds(b, 1), :], (SUBLANES, MLA_KV_RANK)), qabs8)
        qr_h = qrow[:, h * MLA_SLOT + HEAD_DIM:h * MLA_SLOT + HEAD_DIM + MLA_ROPE]
        qr8 = jnp.where(rowr == N_HEADS + h, jnp.broadcast_to(qr_h, (SUBLANES, MLA_ROPE)), qr8)
    qabs8 = qabs8.astype(BF16)
    qr8 = qr8.astype(BF16)
    cnew =cnew_ref[pl.ds(b, 1), :]
    eye8 = (lax.broadcasted_iota(jnp.int32, (SUBLANES, SUBLANES), 0)
            == lax.broadcasted_iota(jnp.int32, (SUBLANES, SUBLANES), 1))
    cnew_col = jnp.sum(jnp.where(eye8, jnp.broadcast_to(cnew, (SUBLANES, SUBLANES)), 0.0),
                       axis=-1, keepdims=True)

    def chunk(ci, carry):
        m, l, acc_f, acc_m = carry
        gidx = b * nc + ci
        slot = gidx % DECODE_RING
        nxt_b, nxt_c = chunk_of(gidx + depth)
        nxt_slot = (gidx + depth) % DECODE_RING

        wait_chunk(b, ci, slot)

        s_list = []
        zero_rows = jnp.zeros((SUBLANES - N_HEADS, page), F32)
        for j in range(g_pages):
            start_page(nxt_b, nxt_c, nxt_slot, j)
            s_f = jnp.sum(kbuf[slot, j] * qbc, axis=1)
            cp_ = cbuf[slot, j].astype(BF16)
            rt = rbuf[slot, j].astype(BF16)
            s = jnp.concatenate([s_f, zero_rows], axis=0) + _dot_nt(qabs8, cp_) + _dot(qr8, rt)
            s_list.append(s + bias_ref[ci * g_pages + j] + cnew_col)
        s = jnp.concatenate(s_list, axis=-1)
        m_new = jnp.maximum(m, jnp.max(s, axis=-1, keepdims=True))
        alpha = jnp.exp(m - m_new)
        p = jnp.exp(s - m_new)
        l = alpha * l + jnp.sum(p, axis=-1, keepdims=True)
        pb = p.astype(BF16)
        acc_f = acc_f * alpha[0:N_HEADS].reshape(N_HEADS, 1, 1)
        om = jnp.zeros((SUBLANES, MLA_KV_RANK), F32)
        for j in range(g_pages):
            pj = p[0:N_HEADS, j * page:(j + 1) * page]
            acc_f = acc_f + vbuf[slot, j] * pj[:, None, :]
            om = om + _dot(pb[:, j * page:(j + 1) * page], cbuf[slot, j].astype(BF16))
        return m_new, l, acc_f, alpha * acc_m + om

    init = (jnp.full((SUBLANES, 1), -jnp.inf, F32), jnp.zeros((SUBLANES, 1), F32),
            jnp.zeros((N_HEADS, HEAD_DIM, page), F32), jnp.zeros((SUBLANES, MLA_KV_RANK), F32))
    m, l, acc_f, acc_m = lax.fori_loop(0, nc, chunk, init)
    acc_row = lax.dot_general(jnp.ones((SUBLANES, page), F32), acc_f.reshape(gw, page),
                              (((1,), (1,)), ((), ())), precision=HIGHEST,
                              preferred_element_type=F32)[0:1, :]

    def head_row(col):
        return jnp.sum(jnp.where(head_mask, col, 0.0), axis=0, keepdims=True)

    knew = knew_ref[pl.ds(b, 1), :].astype(BF16).astype(F32)
    vnew = vnew_ref[pl.ds(b, 1), :].astype(BF16).astype(F32)
    ckvnew = ckvnew_ref[pl.ds(b, 1), :].astype(BF16).astype(F32)
    krnew = krnew_ref[pl.ds(b, 1), :].astype(BF16).astype(F32)
    s_new = (jnp.sum(qf8.astype(F32) * knew, axis=-1, keepdims=True)
             + jnp.sum(qabs8.astype(F32) * ckvnew, axis=-1, keepdims=True)
             + jnp.sum(qr8.astype(F32) * krnew, axis=-1, keepdims=True))
    m_fin = jnp.maximum(m, s_new)
    alpha = jnp.exp(m - m_fin)
    p_new = jnp.exp(s_new - m_fin)
    l = alpha * l + p_new
    p_new = p_new.astype(BF16).astype(F32)
    acc_m = alpha * acc_m + p_new * ckvnew
    inv = 1.0 / l
    fox_ref[pl.ds(b, 1), :] = (head_row(alpha) * acc_row + head_row(p_new) * vnew) * head_row(inv)
    olat = acc_m * inv
    for h in range(N_HEADS):
        olat_ref[h, pl.ds(b, 1), :] = olat[N_HEADS + h:N_HEADS + h + 1, :]

    @pl.when(b == nb - 1)
    def _():
        for g in range(depth):
            wait_chunk(nb_static - 1, nc - 1, (total + g) % DECODE_RING)
        for h in range(N_HEADS):
            mla_ref[:, h * HEAD_DIM:(h + 1) * HEAD_DIM] = _dot(olat_ref[h].astype(BF16), wuv_ref[h])


def _decode_call(page_table, bias, fq, qcat, cnew8, knew, vnew, ckvnew, krnew, wuk_h, wuv_h,
                 k_pool, v_pool, ckv_pool, kr_pool, layer):
    nb, n_pages = page_table.shape
    page = k_pool.shape[-1]
    g_pages = math.gcd(n_pages, DECODE_PAGES_PER_STEP)
    gw = GROUP_WIDTH
    vm = [fq, fq.T, qcat, cnew8, knew, vnew, ckvnew, krnew, wuk_h, wuv_h]
    in_specs = ([pl.BlockSpec((None, n_pages, SUBLANES, page), lambda b, pt: (b, 0, 0, 0))]
                + [pl.BlockSpec(a.shape, lambda b, pt, _n=a.ndim: (0,) * _n) for a in vm]
                + [pl.BlockSpec(memory_space=pl.ANY)] * 4)
    out_full = pl.BlockSpec((nb, gw), lambda b, pt: (0, 0))
    return pl.pallas_call(
        functools.partial(_decode_kernel, g_pages=g_pages, layer=layer),
        out_shape=[jax.ShapeDtypeStruct((nb, gw), F32), jax.ShapeDtypeStruct((nb, gw), F32)],
        grid_spec=pltpu.PrefetchScalarGridSpec(
            num_scalar_prefetch=1,
            grid=(nb,),
            in_specs=in_specs,
            out_specs=[out_full, out_full],
            scratch_shapes=[pltpu.VMEM((DECODE_RING, g_pages, N_HEADS, HEAD_DIM, page), F32),
                            pltpu.VMEM((DECODE_RING, g_pages, N_HEADS, HEAD_DIM, page), F32),
                            pltpu.VMEM((DECODE_RING, g_pages, page, MLA_KV_RANK), F32),
                            pltpu.VMEM((DECODE_RING, g_pages, MLA_ROPE, page), F32),
                            pltpu.SemaphoreType.DMA((4, DECODE_RING)),
                            pltpu.VMEM((N_HEADS, nb, MLA_KV_RANK), F32),
                            pltpu.VMEM((N_HEADS, nb, MLA_KV_RANK), F32)]),
        compiler_params=pltpu.CompilerParams(
            dimension_semantics=("arbitrary",), vmem_limit_bytes=VMEM_LIMIT_BYTES),
        name="decode_attn",
    )(page_table, bias, *vm, k_pool, v_pool, ckv_pool, kr_pool)


def _swap_halves(w):
    half = w.shape[-1] // 2
    return jnp.concatenate([w[..., half:], w[..., :half]], axis=-1)


def _layer_weights(p, l):
    gw = GROUP_WIDTH
    w_in = p['w_in'][l]
    sizes = (gw, gw, gw, N_HEADS, gw, gw, gw, gw, gw, GLA_LOWRANK, MLA_Q_RANK, MLA_KV_RANK, MLA_ROPE)
    offs = np.concatenate([[0], np.cumsum(sizes)])
    fq, fk, fv, ff, su, gq, gk, gv, gg, glr, mq, mkv, mkr = [
        w_in[:, int(offs[i]):int(offs[i + 1])] for i in range(len(sizes))]
    d = w_in.shape[0]
    small = jnp.concatenate([mkr, _swap_halves(mkr), glr, ff,
                             jnp.zeros((d, LANES - 2 * MLA_ROPE - GLA_LOWRANK - N_HEADS), F32)], axis=1)
    wcat = jnp.concatenate([fq, fk, fv, su, gq, gk, gv, gg, mkv, small, mq,
                            jnp.zeros((d, N_ZCOLS - C_MQ - MLA_Q_RANK), F32)], axis=1).astype(BF16)

    w_uq = p['w_mla_uq'][l].reshape(MLA_Q_RANK, N_HEADS, HEAD_DIM + MLA_ROPE)
    nope, ropew = w_uq[..., :HEAD_DIM], w_uq[..., HEAD_DIM:]
    zpad = jnp.zeros((MLA_Q_RANK, N_HEADS, MLA_SLOT - HEAD_DIM - MLA_ROPE), F32)
    w_uq_a = jnp.concatenate([nope, ropew, zpad], axis=-1).reshape(MLA_Q_RANK, N_HEADS * MLA_SLOT)
    w_uq_b = jnp.concatenate([jnp.zeros_like(nope), _swap_halves(ropew), zpad],
                             axis=-1).reshape(MLA_Q_RANK, N_HEADS * MLA_SLOT)
    w_uk = p['w_mla_uk'][l]
    w_uk_slots = jnp.concatenate(
        [w_uk, jnp.zeros((MLA_KV_RANK, N_HEADS, MLA_SLOT - HEAD_DIM), F32)],
        axis=-1).reshape(MLA_KV_RANK, N_HEADS * MLA_SLOT)
    p_kr = jnp.concatenate([jnp.zeros((MLA_ROPE, HEAD_DIM), F32), jnp.eye(MLA_ROPE, dtype=F32),
                            jnp.zeros((MLA_ROPE, MLA_SLOT - HEAD_DIM - MLA_ROPE), F32)], axis=1)
    p_kr = jnp.tile(p_kr, (1, N_HEADS))

    abr, abi, bin_re, bin_im = _s5_discretize(p['s5_a_re'][l], p['s5_a_im'][l], p['s5_log_dt'][l],
                                              p['s5_b_re'][l], p['s5_b_im'][l])
    row = lambda a: a.reshape(1, -1).astype(F32)
    return dict(
        wcat=wcat, norm_mix_pre=row(p['norm_mix_pre'][l]), b_fox_f=row(p['b_fox_f'][l]),
        w_gla_gk=p['w_gla_gk'][l].astype(BF16), b_gla_gk=row(p['b_gla_gk'][l]),
        mla_q_norm=row(p['mla_q_norm'][l]), w_uq_a=w_uq_a.astype(BF16), w_uq_b=w_uq_b.astype(BF16),
        mla_kv_norm=row(p['mla_kv_norm'][l]), w_uk_slots=w_uk_slots.astype(BF16),
        w_uv=p['w_mla_uv'][l].reshape(MLA_KV_RANK, gw).astype(BF16), p_kr=p_kr.astype(BF16),
        w_uk_h=jnp.transpose(w_uk, (1, 0, 2)).astype(BF16),
        w_uv_h=jnp.transpose(p['w_mla_uv'][l], (1, 0, 2)).astype(BF16),
        s5_abar_re=abr, s5_abar_im=abi, s5_bin_re=bin_re, s5_bin_im=bin_im,
        s5_cout_re=_s5_out_blockdiag(p['s5_c_re'][l]), s5_cout_im=_s5_out_blockdiag(p['s5_c_im'][l]),
        s5_d=row(p['s5_d'][l]), w_s5_glu=p['w_s5_glu'][l].astype(BF16), b_s5_glu=row(p['b_s5_glu'][l]),
        gla_norm=row(p['gla_norm'][l]),
        w_out=p['w_out'][l].astype(BF16), norm_mix_post=row(p['norm_mix_post'][l]),
        norm_ffn_pre=row(p['norm_ffn_pre'][l]), norm_ffn_post=row(p['norm_ffn_post'][l]),
        w_ffn_gate=p['w_ffn_gate'][l].astype(BF16), w_ffn_up=p['w_ffn_up'][l].astype(BF16),
        w_ffn_down=p['w_ffn_down'][l].astype(BF16))


def _rope_tables(pos):
    half = MLA_ROPE // 2
    inv = ROPE_THETA ** (-jnp.arange(half, dtype=F32) * 2.0 / MLA_ROPE)
    ang = pos.astype(F32)[:, None] * inv[None, :]
    cos, sin = jnp.cos(ang), jnp.sin(ang)
    n = pos.shape[0]
    c32 = jnp.concatenate([cos, cos], axis=-1)
    s32 = jnp.concatenate([-sin, sin], axis=-1)
    ones = jnp.ones((n, HEAD_DIM), F32)
    zeros = jnp.zeros((n, HEAD_DIM), F32)
    zpad = jnp.zeros((n, MLA_SLOT - HEAD_DIM - MLA_ROPE), F32)
    tqc = jnp.tile(jnp.concatenate([ones, c32, zpad], axis=-1), (1, N_HEADS))
    tqs = jnp.tile(jnp.concatenate([zeros, s32, zpad], axis=-1), (1, N_HEADS))
    return tqc, tqs, jnp.concatenate([c32, s32], axis=-1)


def kernel(x_prompt, x_sample, cache_fox_k, cache_fox_v, cache_fox_logf, cache_mla_ckv, cache_mla_krope, state_s5_re, state_s5_im, state_gla, page_table, meta_tokens, norm_mix_pre, norm_mix_post, norm_ffn_pre, norm_ffn_post, w_in, b_fox_f, s5_a_re, s5_a_im, s5_log_dt, s5_b_re, s5_b_im, s5_c_re, s5_c_im, s5_d, w_s5_glu, b_s5_glu, w_gla_gk, b_gla_gk, gla_norm, mla_q_norm, w_mla_uq, mla_kv_norm, w_mla_uk, w_mla_uv, w_out, w_ffn_gate, w_ffn_up, w_ffn_down):
    params = dict(
        norm_mix_pre=norm_mix_pre, norm_mix_post=norm_mix_post, norm_ffn_pre=norm_ffn_pre,
        norm_ffn_post=norm_ffn_post, w_in=w_in, b_fox_f=b_fox_f, s5_a_re=s5_a_re, s5_a_im=s5_a_im,
        s5_log_dt=s5_log_dt, s5_b_re=s5_b_re, s5_b_im=s5_b_im, s5_c_re=s5_c_re, s5_c_im=s5_c_im,
        s5_d=s5_d, w_s5_glu=w_s5_glu, b_s5_glu=b_s5_glu, w_gla_gk=w_gla_gk, b_gla_gk=b_gla_gk,
        gla_norm=gla_norm, mla_q_norm=mla_q_norm, w_mla_uq=w_mla_uq, mla_kv_norm=mla_kv_norm,
        w_mla_uk=w_mla_uk, w_mla_uv=w_mla_uv, w_out=w_out, w_ffn_gate=w_ffn_gate,
        w_ffn_up=w_ffn_up, w_ffn_down=w_ffn_down)
    depth = w_in.shape[0]
    nbp, seq, d = x_prompt.shape
    nbs, dec_seq, _ = x_sample.shape
    assert dec_seq == 1, "the sample path handles one new token per sequence"
    t = seq + N_META
    n_pages, page = page_table.shape[1], cache_fox_k.shape[2]
    past_len = n_pages * page
    gw = GROUP_WIDTH

    hp = jnp.concatenate([jnp.broadcast_to(meta_tokens.astype(x_prompt.dtype)[None], (nbp, N_META, d)),
                          x_prompt], axis=1)
    hs = x_sample.reshape(1, nbs, d)
    tabs_p = _rope_tables(jnp.arange(t))
    tabs_s = _rope_tables(jnp.full((nbs,), past_len, jnp.int32))
    k_pool_t = jnp.transpose(cache_fox_k, (0, 1, 3, 4, 2))
    v_pool_t = jnp.transpose(cache_fox_v, (0, 1, 3, 4, 2))
    kr_pool_t = jnp.transpose(cache_mla_krope, (0, 1, 3, 2))

    rows_p, rows_s = [], []
    for l in range(depth):
        lw = _layer_weights(params, l)

        (fq, fk, fv, fkb, fvb, logf, c, su_tm, gla_in, qcat, kcat, vmla, ckv, kr) = _proj_call(
            hp, lw, tabs_p, cumsum=True)
        fox_o = _attn_call(fq, fkb, fvb, c)
        mla_o = _attn_call(qcat, kcat, vmla)
        zeros_state = jnp.zeros((nbp, S5_LANES), F32)
        s5_o, s5_re, s5_im = _s5_call(su_tm.reshape(t * nbp, gw), zeros_state, zeros_state, lw, nbp)
        gla_o, gla_st = _gla_call(gla_in, jnp.zeros((nbp, gw, HEAD_DIM), F32), lw['gla_norm'])
        hp = _finish_call(hp, fox_o, s5_o.reshape(t, nbp * gw), gla_o, mla_o, lw)
        rows_p.append((fk.reshape(nbp, t, N_HEADS, HEAD_DIM), fv.reshape(nbp, t, N_HEADS, HEAD_DIM),
                       logf, ckv, kr, s5_re.reshape(nbp, S5_GROUPS, S5_STATE),
                       s5_im.reshape(nbp, S5_GROUPS, S5_STATE),
                       jnp.swapaxes(gla_st.reshape(nbp, N_HEADS, HEAD_DIM, HEAD_DIM), 2, 3)))

        (fq, fk, fv, fkb, fvb, logf, c, su_tm, gla_in, qcat, kcat, vmla, ckv, kr) = _proj_call(
            hs, lw, tabs_s, cumsum=False)
        logf_t = jnp.pad(jnp.swapaxes(cache_fox_logf[l].astype(F32), 1, 2),
                         ((0, 0), (0, SUBLANES - N_HEADS), (0, 0)))
        bias = _fox_bias_call(page_table, logf_t)
        cnew8 = jnp.concatenate([logf[0], jnp.zeros((nbs, SUBLANES - N_HEADS), F32)], axis=1)
        fox_o, mla_o = _decode_call(
            page_table, bias, fq[0].astype(F32), qcat[0].astype(F32), cnew8, fk[0], fv[0], ckv[0], kr[0],
            lw['w_uk_h'], lw['w_uv_h'], k_pool_t, v_pool_t, cache_mla_ckv, kr_pool_t, l)
        s5_o, s5_re, s5_im = _s5_call(su_tm, state_s5_re[l].reshape(nbs, S5_LANES),
                                      state_s5_im[l].reshape(nbs, S5_LANES), lw, nbs)
        gla_pad = jnp.concatenate([jnp.zeros((nbs, SUBLANES - 1, 5 * gw), F32),
                                   gla_in.reshape(nbs, 1, 5 * gw)], axis=1)
        s0t = jnp.swapaxes(state_gla[l].astype(F32), 2, 3).reshape(nbs, gw, HEAD_DIM)
        gla_o, gla_st = _gla_call(gla_pad, s0t, lw['gla_norm'])
        gla_o = gla_o[:, SUBLANES - 1, :].reshape(1, nbs, gw)
        hs = _finish_call(hs, fox_o.reshape(1, nbs, gw), s5_o, gla_o, mla_o.reshape(1, nbs, gw), lw)
        rows_s.append((fk.reshape(nbs, 1, N_HEADS, HEAD_DIM), fv.reshape(nbs, 1, N_HEADS, HEAD_DIM),
                       logf.reshape(nbs, 1, N_HEADS), ckv.reshape(nbs, 1, MLA_KV_RANK),
                       kr.reshape(nbs, 1, MLA_ROPE), s5_re.reshape(nbs, S5_GROUPS, S5_STATE),
                       s5_im.reshape(nbs, S5_GROUPS, S5_STATE),
                       jnp.swapaxes(gla_st.reshape(nbs, N_HEADS, HEAD_DIM, HEAD_DIM), 2, 3)))

    fk_p, fv_p, flf_p, ckv_p, kr_p, s5re_p, s5im_p, gla_p = [jnp.stack(r) for r in zip(*rows_p)]
    fk_s, fv_s, flf_s, ckv_s, kr_s, s5re_s, s5im_s, gla_s = [jnp.stack(r) for r in zip(*rows_s)]
    y_prompt = hp[:, N_META:]
    y_sample = hs.reshape(nbs, 1, d)
    return (y_prompt, y_sample, fk_p, fk_s, fv_p, fv_s, flf_p, flf_s, ckv_p, ckv_s, kr_p, kr_s,
            s5re_p, s5re_s, s5im_p, s5im_s, gla_p, gla_s)
```

```python
import functools
import math

import jax
import jax.numpy as jnp
import numpy as np
from jax import lax
from jax.experimental import pallas as pl
from jax.experimental.pallas import tpu as pltpu

F32 = jnp.float32
BF16 = jnp.bfloat16
HIGHEST = lax.Precision.HIGHEST

N_META = 16
HEAD_DIM = 64
NORM_EPS = 1e-6
N_HEADS = 4
GROUP_WIDTH = N_HEADS * HEAD_DIM
S5_GC = 16
S5_GROUPS = GROUP_WIDTH // S5_GC
S5_STATE = 64
S5_LANES = S5_GROUPS * S5_STATE
GLA_LOWRANK = 16
GLA_GATE_NORM = 16.0
MLA_ROPE = HEAD_DIM // 2
MLA_Q_RANK = 3 * GROUP_WIDTH // 4
MLA_KV_RANK = GROUP_WIDTH // 2
FOX_SCALE = HEAD_DIM ** -0.5
GLA_SCALE = HEAD_DIM ** -0.5
MLA_SCALE = (HEAD_DIM + MLA_ROPE) ** -0.5
ROPE_THETA = 10000.0
LOG2E = math.log2(math.e)
MLA_SLOT = 128

SUBLANES = 8
LANES = 128
VMEM_LIMIT_BYTES = 56 * 1024 * 1024

PROJ_TILE = 688
ATTN_TILE = 384
S5_CHUNK = 48
GLA_CHUNK = 48
GLA_BATCH_BLOCK = 8
FFN_TILE = 688
DECODE_PAGES_PER_STEP = 16
DECODE_RING = 3
BIAS_RING = 3

C_FOX = 0
C_S5 = 768
C_GLA = 1024
C_MKV = 2048
C_SMALL = 2176
C_MQ = 2304
N_ZCOLS = 2560


def _pick_tile(n, target):
    best = None
    for t in range(SUBLANES, min(n, target) + 1, SUBLANES):
        if n % t == 0:
            best = t
    return best if best is not None else n


def _rms(x, w):
    return x * lax.rsqrt(jnp.mean(x * x, axis=-1, keepdims=True) + NORM_EPS) * w


def _log_sigmoid(x):
    return jnp.minimum(x, 0.0) - jnp.log1p(jnp.exp(-jnp.abs(x)))


def _sigmoid(x):
    return 1.0 / (1.0 + jnp.exp(-x))


def _dot(a, b):
    return jnp.dot(a, b, preferred_element_type=F32)


def _dot_nt(a, b):
    return lax.dot_general(a, b, (((1,), (1,)), ((), ())), preferred_element_type=F32)


def _dot_tn(a, b):
    return lax.dot_general(a, b, (((0,), (0,)), ((), ())), preferred_element_type=F32)


def _bf16_pieces(x):
    hi = x.astype(BF16)
    r1 = x - hi.astype(F32)
    mid = r1.astype(BF16)
    lo = (r1 - mid.astype(F32)).astype(BF16)
    return hi, mid, lo


def _dot_exact_lhs(a01, x):
    hi, mid, lo = _bf16_pieces(x)
    return _dot(a01, hi) + _dot(a01, mid) + _dot(a01, lo)


def _const_spec(shape):
    nd = len(shape)
    return pl.BlockSpec(shape, lambda *_: (0,) * nd)


def _proj_kernel(x_ref, nw_ref, w_ref, bff_ref, wgk_ref, bgk_ref, qnw_ref, wuqa_ref, wuqb_ref,
                 kvnw_ref, wuk_ref, wuv_ref, pkr_ref, tqc_ref, tqs_ref, tkr_ref,
                 fq_ref, fk_ref, fv_ref, fkb_ref, fvb_ref, logf_ref, c_ref, su_ref, gla_ref,
                 qcat_ref, kcat_ref, vmla_ref, ckv_ref, kr_ref, carry_ref, *, cumsum):
    tm = x_ref.shape[0]
    hn = _rms(x_ref[...], nw_ref[...]).astype(BF16)

    zf = _dot(hn, w_ref[:, C_FOX:C_FOX + 3 * GROUP_WIDTH])
    fq_ref[...] = (zf[:, :GROUP_WIDTH] * FOX_SCALE).astype(BF16)
    fk = zf[:, GROUP_WIDTH:2 * GROUP_WIDTH]
    fv = zf[:, 2 * GROUP_WIDTH:]
    fk_ref[...] = fk
    fv_ref[...] = fv
    fkb_ref[...] = fk.astype(BF16)
    fvb_ref[...] = fv.astype(BF16)

    su_ref[...] = _dot(hn, w_ref[:, C_S5:C_S5 + GROUP_WIDTH])

    zs = _dot(hn, w_ref[:, C_SMALL:C_SMALL + LANES])
    mkr = zs[:, 0:MLA_ROPE]
    mkr_sw = zs[:, MLA_ROPE:2 * MLA_ROPE]
    glr = zs[:, 2 * MLA_ROPE:2 * MLA_ROPE + GLA_LOWRANK]
    ff = zs[:, 2 * MLA_ROPE + GLA_LOWRANK:2 * MLA_ROPE + GLA_LOWRANK + N_HEADS]

    logf = _log_sigmoid(ff + bff_ref[...])
    logf_ref[...] = logf
    if cumsum:
        ti = pl.program_id(1)

        @pl.when(ti == 0)
        def _():
            carry_ref[...] = jnp.zeros_like(carry_ref)

        row = lax.broadcasted_iota(jnp.int32, (tm, tm), 0)
        col = lax.broadcasted_iota(jnp.int32, (tm, tm), 1)
        tri = jnp.where(row >= col, 1.0, 0.0).astype(BF16)
        c = _dot_exact_lhs(tri, logf) + carry_ref[...]
        c_ref[...] = c
        carry_ref[...] = c[tm - 1:tm, :]
    else:
        c_ref[...] = logf

    tkr = tkr_ref[...]
    kr = mkr * tkr[:, :MLA_ROPE] + mkr_sw * tkr[:, MLA_ROPE:]
    kr_ref[...] = kr

    glog = _log_sigmoid(_dot(glr.astype(BF16), wgk_ref[...]) + bgk_ref[...]) * (1.0 / GLA_GATE_NORM)
    zg = _dot(hn, w_ref[:, C_GLA:C_GLA + 4 * GROUP_WIDTH])
    gla_ref[:, 0:GROUP_WIDTH] = zg[:, 0:GROUP_WIDTH] * GLA_SCALE
    gla_ref[:, GROUP_WIDTH:3 * GROUP_WIDTH] = zg[:, GROUP_WIDTH:3 * GROUP_WIDTH]
    gla_ref[:, 3 * GROUP_WIDTH:4 * GROUP_WIDTH] = glog
    gla_ref[:, 4 * GROUP_WIDTH:5 * GROUP_WIDTH] = zg[:, 3 * GROUP_WIDTH:]

    mkv = _dot(hn, w_ref[:, C_MKV:C_MKV + MLA_KV_RANK])
    ckv = _rms(mkv, kvnw_ref[...])
    ckv_ref[...] = ckv
    ckv_b = ckv.astype(BF16)
    kcat = _dot(ckv_b, wuk_ref[...]) + _dot(kr.astype(BF16), pkr_ref[...])
    kcat_ref[...] = kcat.astype(BF16)
    vmla_ref[...] = _dot(ckv_b, wuv_ref[...]).astype(BF16)

    mq = _dot(hn, w_ref[:, C_MQ:C_MQ + MLA_Q_RANK])
    qln = _rms(mq, qnw_ref[...]).astype(BF16)
    qcat = _dot(qln, wuqa_ref[...]) * tqc_ref[...] + _dot(qln, wuqb_ref[...]) * tqs_ref[...]
    qcat_ref[...] = (qcat * MLA_SCALE).astype(BF16)


def _proj_call(x3, lw, tabs, *, cumsum):
    nb, t, d = x3.shape
    tm = _pick_tile(t, PROJ_TILE)
    nt = t // tm
    tqc, tqs, tkr = tabs
    gw = GROUP_WIDTH
    qs = N_HEADS * MLA_SLOT

    def tok(width):
        return pl.BlockSpec((None, tm, width), lambda b, i: (b, i, 0))

    def tab(width):
        return pl.BlockSpec((tm, width), lambda b, i: (i, 0))

    consts = [lw['norm_mix_pre'], lw['wcat'], lw['b_fox_f'], lw['w_gla_gk'], lw['b_gla_gk'],
              lw['mla_q_norm'], lw['w_uq_a'], lw['w_uq_b'], lw['mla_kv_norm'], lw['w_uk_slots'],
              lw['w_uv'], lw['p_kr']]
    in_specs = [tok(d)] + [_const_spec(c.shape) for c in consts] + [tab(qs), tab(qs), tab(2 * MLA_ROPE)]
    out_shape = [
        jax.ShapeDtypeStruct((nb, t, gw), BF16),
        jax.ShapeDtypeStruct((nb, t, gw), F32),
        jax.ShapeDtypeStruct((nb, t, gw), F32),
        jax.ShapeDtypeStruct((nb, t, gw), BF16),
        jax.ShapeDtypeStruct((nb, t, gw), BF16),
        jax.ShapeDtypeStruct((nb, t, N_HEADS), F32),
        jax.ShapeDtypeStruct((nb, t, N_HEADS), F32),
        jax.ShapeDtypeStruct((t, nb * gw), F32),
        jax.ShapeDtypeStruct((nb, t, 5 * gw), F32),
        jax.ShapeDtypeStruct((nb, t, qs), BF16),
        jax.ShapeDtypeStruct((nb, t, qs), BF16),
        jax.ShapeDtypeStruct((nb, t, gw), BF16),
        jax.ShapeDtypeStruct((nb, t, MLA_KV_RANK), F32),
        jax.ShapeDtypeStruct((nb, t, MLA_ROPE), F32),
    ]
    out_specs = [tok(gw), tok(gw), tok(gw), tok(gw), tok(gw), tok(N_HEADS), tok(N_HEADS),
                 pl.BlockSpec((tm, gw), lambda b, i: (i, b)),
                 tok(5 * gw), tok(qs), tok(qs), tok(gw), tok(MLA_KV_RANK), tok(MLA_ROPE)]
    return pl.pallas_call(
        functools.partial(_proj_kernel, cumsum=cumsum),
        out_shape=out_shape,
        grid=(nb, nt),
        in_specs=in_specs,
        out_specs=out_specs,
        scratch_shapes=[pltpu.VMEM((1, N_HEADS), F32)],
        compiler_params=pltpu.CompilerParams(
            dimension_semantics=("arbitrary", "arbitrary"), vmem_limit_bytes=VMEM_LIMIT_BYTES),
        name="proj",
    )(x3, *consts, tqc, tqs, tkr)


def _attn_kernel(*refs, dk, has_bias):
    if has_bias:
        q_ref, k_ref, v_ref, c_ref, o_ref, qp_ref, kp_ref, vp_ref, s_ref, cp_ref = refs
    else:
        q_ref, k_ref, v_ref, o_ref, qp_ref, kp_ref, vp_ref, s_ref = refs
    t = k_ref.shape[0]
    tq = o_ref.shape[0]
    tp = kp_ref.shape[0]
    qi = pl.program_id(1)

    @pl.when(qi == 0)
    def _():
        pairs = [(q_ref, qp_ref, None), (k_ref, kp_ref, None), (v_ref, vp_ref, None)]
        if has_bias:
            pairs.append((c_ref, cp_ref, LOG2E))
        for src, dst, scale in pairs:
            dst[0:t, :] = src[...] if scale is None else src[...] * scale
            dst[t:tp, :] = jnp.zeros((tp - t, dst.shape[1]), dst.dtype)

    q0 = pl.multiple_of(qi * tq, tq)
    causal = (lax.broadcasted_iota(jnp.int32, (tq, tq), 0)
              <= lax.broadcasted_iota(jnp.int32, (tq, tq), 1))

    qs = [qp_ref[pl.ds(q0, tq), h * dk:(h + 1) * dk] for h in range(N_HEADS)]

    groups = tq // SUBLANES

    def fold(x):
        return x.reshape(groups, SUBLANES, tq)

    def score_tile(h, k0, kj, masked, mx):
        s = _dot_nt(kp_ref[pl.ds(k0, tq), h * dk:(h + 1) * dk], qs[h]) * LOG2E
        if has_bias:
            s = s - cp_ref[pl.ds(k0, tq), h:h + 1]
        if masked:
            s = jnp.where(causal, s, -jnp.inf)
        s_ref[h, kj] = s
        return jnp.maximum(mx, jnp.max(fold(s), axis=0))

    def pass1(kj, mxs):
        k0 = pl.multiple_of(kj * tq, tq)
        return tuple(score_tile(h, k0, kj, False, mxs[h]) for h in range(N_HEADS))

    mxs = lax.fori_loop(0, qi, pass1,
                        tuple(jnp.full((SUBLANES, tq), -jnp.inf, F32) for _ in range(N_HEADS)))
    ms = [jnp.max(score_tile(h, q0, qi, True, mxs[h]), axis=0, keepdims=True) for h in range(N_HEADS)]

    def pass2(kj, carry):
        k0 = pl.multiple_of(kj * tq, tq)
        new = []
        for h in range(N_HEADS):
            lsum, acc = carry[h]
            p = jnp.exp2(s_ref[h, kj] - ms[h])
            v = vp_ref[pl.ds(k0, tq), h * HEAD_DIM:(h + 1) * HEAD_DIM]
            new.append((lsum + jnp.sum(fold(p), axis=0),
                        acc + _dot_tn(v, p.astype(BF16))))
        return tuple(new)

    carry = lax.fori_loop(0, qi + 1, pass2,
                          tuple((jnp.zeros((SUBLANES, tq), F32), jnp.zeros((HEAD_DIM, tq), F32))
                                for _ in range(N_HEADS)))
    outs = [acc / jnp.sum(lsum, axis=0, keepdims=True) for (lsum, acc) in carry]
    for pair in range(N_HEADS // 2):
        o_ref[:, pair * 2 * HEAD_DIM:(pair + 1) * 2 * HEAD_DIM] = jnp.concatenate(
            outs[2 * pair:2 * pair + 2], axis=0).T


def _attn_call(q, k, v, c=None):
    nb, t, qw = q.shape
    dk = qw // N_HEADS
    tq = ATTN_TILE
    nq = pl.cdiv(t, tq)
    tp = nq * tq
    has_bias = c is not None

    def seq(width):
        return pl.BlockSpec((None, t, width), lambda b, i: (b, 0, 0))

    in_specs = [seq(qw), seq(qw), seq(GROUP_WIDTH)]
    args = [q, k, v]
    scratch = [pltpu.VMEM((tp, qw), BF16), pltpu.VMEM((tp, qw), BF16), pltpu.VMEM((tp, GROUP_WIDTH), BF16),
               pltpu.VMEM((N_HEADS, nq, tq, tq), F32)]
    if has_bias:
        in_specs.append(seq(N_HEADS))
        args.append(c)
        scratch.append(pltpu.VMEM((tp, N_HEADS), F32))
    return pl.pallas_call(
        functools.partial(_attn_kernel, dk=dk, has_bias=has_bias),
        out_shape=jax.ShapeDtypeStruct((nb, tp, GROUP_WIDTH), F32),
        grid=(nb, nq),
        in_specs=in_specs,
        out_specs=pl.BlockSpec((None, tq, GROUP_WIDTH), lambda b, i: (b, i, 0)),
        scratch_shapes=scratch,
        compiler_params=pltpu.CompilerParams(
            dimension_semantics=("arbitrary", "arbitrary"), vmem_limit_bytes=VMEM_LIMIT_BYTES),
        name="fox_attn" if has_bias else "mla_attn",
    )(*args)


def _s5_disc_kernel(are_ref, aim_ref, ldt_ref, bre_ref, bim_ref,
                    abr_ref, abi_ref, bbr_ref, bbi_ref):
    dt = jnp.exp(ldt_ref[...])
    ar = are_ref[...]
    ai = aim_ref[...]
    mag = jnp.exp(ar * dt)
    abr = mag * jnp.cos(ai * dt)
    abi = mag * jnp.sin(ai * dt)
    nr = abr - 1.0
    ni = abi
    den = ar * ar + ai * ai
    fr = (nr * ar + ni * ai) / den
    fi = (ni * ar - nr * ai) / den
    abr_ref[...] = abr
    abi_ref[...] = abi
    for c in range(S5_GC):
        br = bre_ref[c]
        bi = bim_ref[c]
        bbr_ref[c] = fr * br - fi * bi
        bbi_ref[c] = fr * bi + fi * br


def _s5_discretize(a_re, a_im, log_dt, b_re, b_im):
    g, p = a_re.shape
    bt = lambda b: jnp.transpose(b, (2, 0, 1))
    outs = pl.pallas_call(
        _s5_disc_kernel,
        out_shape=[jax.ShapeDtypeStruct((g, p), F32), jax.ShapeDtypeStruct((g, p), F32),
                   jax.ShapeDtypeStruct((S5_GC, g, p), F32), jax.ShapeDtypeStruct((S5_GC, g, p), F32)],
        name="s5_discretize",
    )(a_re, a_im, log_dt.reshape(g, 1), bt(b_re), bt(b_im))
    abr, abi, bbr, bbi = outs
    eye = jnp.eye(g, dtype=F32)

    def in_blockdiag(bb):
        m = jnp.transpose(bb, (1, 0, 2))[:, :, None, :] * eye[:, None, :, None]
        return m.reshape(g * S5_GC, g * p).astype(BF16)

    return abr.reshape(1, g * p), abi.reshape(1, g * p), in_blockdiag(bbr), in_blockdiag(bbi)


def _s5_out_blockdiag(c):
    g = c.shape[0]
    eye = jnp.eye(g, dtype=F32)
    m = jnp.transpose(c, (0, 2, 1))[:, :, None, :] * eye[:, None, :, None]
    return m.reshape(g * c.shape[2], g * c.shape[1]).astype(BF16)


def _gelu_tanh(x):
    return 0.5 * x * (1.0 + jnp.tanh(math.sqrt(2.0 / math.pi) * (x + 0.044715 * (x * x * x))))


def _s5_kernel(u_ref, h0r_ref, h0i_ref, ar_ref, ai_ref, bre_ref, bim_ref, cre_ref, cim_ref,
               d_ref, wglu_ref, bglu_ref, o_ref, hr_ref, hi_ref, sr_ref, si_ref, *, nb):
    i = pl.program_id(0)
    tc = u_ref.shape[0] // nb

    @pl.when(i == 0)
    def _():
        hr_ref[...] = h0r_ref[...]
        hi_ref[...] = h0i_ref[...]

    u = u_ref[...]
    ub = u.astype(BF16)
    sr_ref[...] = _dot(ub, bre_ref[...])
    si_ref[...] = _dot(ub, bim_ref[...])
    ar = ar_ref[...]
    ai = ai_ref[...]

    def step(t, carry):
        hr, hi = carry
        r0 = pl.multiple_of(t * nb, nb)
        nhr = ar * hr - ai * hi + sr_ref[pl.ds(r0, nb), :]
        nhi = ar * hi + ai * hr + si_ref[pl.ds(r0, nb), :]
        sr_ref[pl.ds(r0, nb), :] = nhr
        si_ref[pl.ds(r0, nb), :] = nhi
        return nhr, nhi

    hr, hi = lax.fori_loop(0, tc, step, (hr_ref[...], hi_ref[...]))
    hr_ref[...] = hr
    hi_ref[...] = hi

    y = (_dot(sr_ref[...].astype(BF16), cre_ref[...]) - _dot(si_ref[...].astype(BF16), cim_ref[...])
         + d_ref[...] * u)
    g = _gelu_tanh(y)
    o_ref[...] = g * _sigmoid(_dot(g.astype(BF16), wglu_ref[...]) + bglu_ref[...])


def _s5_call(u_tm, h0r, h0i, lw, nb):
    rows = u_tm.shape[0]
    t = rows // nb
    tc = _pick_tile(t, S5_CHUNK) if t >= SUBLANES else t
    r = tc * nb
    consts = [lw['s5_abar_re'], lw['s5_abar_im'], lw['s5_bin_re'], lw['s5_bin_im'],
              lw['s5_cout_re'], lw['s5_cout_im'], lw['s5_d'], lw['w_s5_glu'], lw['b_s5_glu']]
    state_spec = _const_spec((nb, S5_LANES))
    return pl.pallas_call(
        functools.partial(_s5_kernel, nb=nb),
        out_shape=[jax.ShapeDtypeStruct((rows, GROUP_WIDTH), F32),
                   jax.ShapeDtypeStruct((nb, S5_LANES), F32),
                   jax.ShapeDtypeStruct((nb, S5_LANES), F32)],
        grid=(t // tc,),
        in_specs=[pl.BlockSpec((r, GROUP_WIDTH), lambda i: (i, 0)), state_spec, state_spec]
                 + [_const_spec(c.shape) for c in consts],
        out_specs=[pl.BlockSpec((r, GROUP_WIDTH), lambda i: (i, 0)), state_spec, state_spec],
        scratch_shapes=[pltpu.VMEM((r, S5_LANES), F32), pltpu.VMEM((r, S5_LANES), F32)],
        compiler_params=pltpu.CompilerParams(
            dimension_semantics=("arbitrary",), vmem_limit_bytes=VMEM_LIMIT_BYTES),
        name="s5_scan",
    )(u_tm, h0r, h0i, *consts)


def _gla_kernel(x_ref, s0_ref, nw_ref, o_ref, st_ref):
    ci = pl.program_id(1)
    bb, c = x_ref.shape[0], x_ref.shape[1]
    gw = GROUP_WIDTH

    @pl.when(ci == 0)
    def _():
        st_ref[...] = s0_ref[...]

    row = lax.broadcasted_iota(jnp.int32, (c, c), 0)
    col = lax.broadcasted_iota(jnp.int32, (c, c), 1)
    causal = row >= col
    tri = jnp.where(causal, 1.0, 0.0).astype(BF16)

    for bi in range(bb):
        q = x_ref[bi, :, 0:gw]
        k = x_ref[bi, :, gw:2 * gw]
        v = x_ref[bi, :, 2 * gw:3 * gw]
        g = x_ref[bi, :, 3 * gw:4 * gw]
        gate = x_ref[bi, :, 4 * gw:5 * gw]
        bc = _dot_exact_lhs(tri, g)
        b_last = bc[c - 1:c, :]
        b_mid = bc[c // 2:c // 2 + 1, :]
        q_in = (q * jnp.exp(bc - b_mid)).astype(BF16)
        k_in = (k * jnp.exp(b_mid - bc)).astype(BF16)
        q_st = (q * jnp.exp(bc)).astype(BF16)
        k_st = (k * jnp.exp(b_last - bc)).astype(BF16)
        vb = v.astype(BF16)
        decay = jnp.exp(b_last)
        for h in range(N_HEADS):
            sl = slice(h * HEAD_DIM, (h + 1) * HEAD_DIM)
            att = jnp.where(causal, _dot_nt(q_in[:, sl], k_in[:, sl]), 0.0)
            st = st_ref[bi, sl, :]
            o = _dot_nt(q_st[:, sl], st.astype(BF16)) + _dot(att.astype(BF16), vb[:, sl])
            st_ref[bi, sl, :] = st * decay[:, sl] + _dot_tn(vb[:, sl], k_st[:, sl])
            o = _rms(o, nw_ref[...])
            gt = gate[:, sl]
            o_ref[bi, :, sl] = o * (gt * _sigmoid(gt))


def _gla_call(gla_in, s0t, norm_w):
    nb, t, w = gla_in.shape
    c = _pick_tile(t, GLA_CHUNK)
    bb = math.gcd(nb, GLA_BATCH_BLOCK)
    return pl.pallas_call(
        _gla_kernel,
        out_shape=[jax.ShapeDtypeStruct((nb, t, GROUP_WIDTH), F32),
                   jax.ShapeDtypeStruct((nb, GROUP_WIDTH, HEAD_DIM), F32)],
        grid=(nb // bb, t // c),
        in_specs=[pl.BlockSpec((bb, c, w), lambda b, i: (b, i, 0)),
                  pl.BlockSpec((bb, GROUP_WIDTH, HEAD_DIM), lambda b, i: (b, 0, 0)),
                  _const_spec(norm_w.shape)],
        out_specs=[pl.BlockSpec((bb, c, GROUP_WIDTH), lambda b, i: (b, i, 0)),
                   pl.BlockSpec((bb, GROUP_WIDTH, HEAD_DIM), lambda b, i: (b, 0, 0))],
        compiler_params=pltpu.CompilerParams(
            dimension_semantics=("arbitrary", "arbitrary"), vmem_limit_bytes=VMEM_LIMIT_BYTES),
        name="gla",
    )(gla_in, s0t, norm_w)


def _finish_kernel(x_ref, fox_ref, s5_ref, gla_ref, mla_ref, wout_ref, npost_ref, npre_ref,
                   nffn_ref, wg_ref, wu_ref, wd_ref, o_ref):
    gw = GROUP_WIDTH
    mix = (_dot(fox_ref[...].astype(BF16), wout_ref[0:gw, :])
           + _dot(s5_ref[...].astype(BF16), wout_ref[gw:2 * gw, :])
           + _dot(gla_ref[...].astype(BF16), wout_ref[2 * gw:3 * gw, :])
           + _dot(mla_ref[...].astype(BF16), wout_ref[3 * gw:4 * gw, :]))
    x1 = x_ref[...] + _rms(mix, npost_ref[...])
    hf = _rms(x1, npre_ref[...]).astype(BF16)
    gate = _dot(hf, wg_ref[...])
    act = (gate * _sigmoid(gate) * _dot(hf, wu_ref[...])).astype(BF16)
    f = _dot(act, wd_ref[...])
    o_ref[...] = x1 + _rms(f, nffn_ref[...])


def _finish_call(x3, fox_o, s5_tm, gla_o, mla_o, lw):
    nb, t, d = x3.shape
    tm = _pick_tile(t, FFN_TILE)
    gw = GROUP_WIDTH

    def tok(width):
        return pl.BlockSpec((None, tm, width), lambda b, i: (b, i, 0))

    consts = [lw['w_out'], lw['norm_mix_post'], lw['norm_ffn_pre'], lw['norm_ffn_post'],
              lw['w_ffn_gate'], lw['w_ffn_up'], lw['w_ffn_down']]
    const_specs = [pl.BlockSpec(c.shape, lambda b, i: (0, 0), pipeline_mode=pl.Buffered(1))
                   for c in consts]
    return pl.pallas_call(
        _finish_kernel,
        out_shape=jax.ShapeDtypeStruct((nb, t, d), F32),
        grid=(nb, t // tm),
        in_specs=[tok(d), tok(gw), pl.BlockSpec((tm, gw), lambda b, i: (i, b)), tok(gw), tok(gw)]
                 + const_specs,
        out_specs=tok(d),
        compiler_params=pltpu.CompilerParams(
            dimension_semantics=("arbitrary", "arbitrary"), vmem_limit_bytes=VMEM_LIMIT_BYTES),
        name="mix_ffn",
    )(x3, fox_o, s5_tm, gla_o, mla_o, *consts)


def _fox_bias_kernel(pt_ref, logf_hbm, o_ref, buf_ref, sem_ref):
    b = pl.program_id(0)
    nb = pl.num_programs(0)
    n_pages = buf_ref.shape[1]
    page = buf_ref.shape[3]
    rows = n_pages * SUBLANES

    def page_copy(bb, slot, p):
        return pltpu.make_async_copy(logf_hbm.at[pt_ref[bb, p]], buf_ref.at[slot, p], sem_ref.at[slot])

    def start_all(bb, slot):
        for p in range(n_pages):
            page_copy(bb, slot, p).start()

    depth = BIAS_RING - 1

    @pl.when(b == 0)
    def _():
        for g in range(depth):
            start_all(jnp.minimum(g, nb - 1), g % BIAS_RING)

    slot = b % BIAS_RING
    start_all(jnp.minimum(b + depth, nb - 1), (b + depth) % BIAS_RING)
    for p in range(n_pages):
        page_copy(b, slot, p).wait()

    @pl.when(b == nb - 1)
    def _():
        for g in range(depth):
            for p in range(n_pages):
                page_copy(nb - 1, (nb + g) % BIAS_RING, p).wait()

    x = buf_ref[slot].reshape(rows, page)
    hi, mid, lo = _bf16_pieces(x)
    j = lax.broadcasted_iota(jnp.int32, (page, 2 * page), 0)
    s = lax.broadcasted_iota(jnp.int32, (page, 2 * page), 1)
    w = jnp.where((j > s) | (s >= page), 1.0, 0.0).astype(BF16)
    acc = (_dot(hi, w) + _dot(mid, w) + _dot(lo, w)).reshape(n_pages, SUBLANES, 2 * page)
    run = jnp.zeros((SUBLANES, page), F32)
    for p in reversed(range(n_pages)):
        o_ref[p] = acc[p, :, :page] + run
        run = run + acc[p, :, page:]


def _fox_bias_call(page_table, logf_t):
    nb, n_pages = page_table.shape
    page = logf_t.shape[2]
    rows = n_pages * SUBLANES
    return pl.pallas_call(
        _fox_bias_kernel,
        out_shape=jax.ShapeDtypeStruct((nb, n_pages, SUBLANES, page), F32),
        grid_spec=pltpu.PrefetchScalarGridSpec(
            num_scalar_prefetch=1,
            grid=(nb,),
            in_specs=[pl.BlockSpec(memory_space=pl.ANY)],
            out_specs=pl.BlockSpec((None, n_pages, SUBLANES, page), lambda b, pt: (b, 0, 0, 0)),
            scratch_shapes=[pltpu.VMEM((BIAS_RING, n_pages, SUBLANES, page), F32),
                            pltpu.SemaphoreType.DMA((BIAS_RING,))]),
        compiler_params=pltpu.CompilerParams(
            dimension_semantics=("arbitrary",), vmem_limit_bytes=VMEM_LIMIT_BYTES),
        name="fox_decode_bias",
    )(page_table, logf_t)


def _decode_kernel(pt_ref, bias_ref, fq_ref, fqt_ref, qcat_ref, cnew_ref, knew_ref, vnew_ref, ckvnew_ref,
                   krnew_ref, wuk_ref, wuv_ref, k_hbm, v_hbm, ckv_hbm, kr_hbm,
                   fox_ref, mla_ref,
                   kbuf, vbuf, cbuf, rbuf, sem_ref, qabs_ref, olat_ref, *, g_pages, layer):
    b = pl.program_id(0)
    nb = pl.num_programs(0)
    n_pages = bias_ref.shape[0]
    nc = n_pages // g_pages
    page = bias_ref.shape[2]
    gw = GROUP_WIDTH
    nb_static = fq_ref.shape[0]

    def page_copies(bb, ci, slot, j):
        pid = pt_ref[bb, ci * g_pages + j]
        return [pltpu.make_async_copy(k_hbm.at[layer, pid], kbuf.at[slot, j], sem_ref.at[0, slot]),
                pltpu.make_async_copy(v_hbm.at[layer, pid], vbuf.at[slot, j], sem_ref.at[1, slot]),
                pltpu.make_async_copy(ckv_hbm.at[layer, pid], cbuf.at[slot, j], sem_ref.at[2, slot]),
                pltpu.make_async_copy(kr_hbm.at[layer, pid], rbuf.at[slot, j], sem_ref.at[3, slot])]

    def start_page(bb, ci, slot, j):
        for cp in page_copies(bb, ci, slot, j):
            cp.start()

    def wait_chunk(bb, ci, slot):
        for j in range(g_pages):
            for cp in page_copies(bb, ci, slot, j):
                cp.wait()

    total = nb_static * nc
    depth = DECODE_RING - 1

    def chunk_of(g):
        g = jnp.minimum(g, total - 1)
        return g // nc, g % nc

    @pl.when(b == 0)
    def _():
        for g in range(depth):
            for j in range(g_pages):
                start_page(*chunk_of(g), g % DECODE_RING, j)
        for h in range(N_HEADS):
            qn = qcat_ref[:, h * MLA_SLOT:h * MLA_SLOT + HEAD_DIM].astype(BF16)
            qabs_ref[h] = _dot_nt(qn, wuk_ref[h])

    row8 = lax.broadcasted_iota(jnp.int32, (SUBLANES, gw), 0)
    lane8 = lax.broadcasted_iota(jnp.int32, (SUBLANES, gw), 1)
    head_mask = row8 == lane8 // HEAD_DIM
    qf8 = jnp.where(head_mask, jnp.broadcast_to(fq_ref[pl.ds(b, 1), :], (SUBLANES, gw)),
                    0.0).astype(BF16)
    pick = (lax.broadcasted_iota(jnp.int32, (nb_static, page), 0) == b).astype(BF16)
    qbc = _dot(fqt_ref[...].astype(BF16), pick).reshape(N_HEADS, HEAD_DIM, page)
    rowk = lax.broadcasted_iota(jnp.int32, (SUBLANES, MLA_KV_RANK), 0)
    qabs8 = jnp.zeros((SUBLANES, MLA_KV_RANK), F32)
    rowr = lax.broadcasted_iota(jnp.int32, (SUBLANES, MLA_ROPE), 0)
    qr8 = jnp.zeros((SUBLANES, MLA_ROPE), F32)
    qrow = qcat_ref[pl.ds(b, 1), :]
    for h in range(N_HEADS):
        qabs8 = jnp.where(rowk == N_HEADS + h,
                          jnp.broadcast_to(qabs_ref[h, pl.ds(b, 1), :], (SUBLANES, MLA_KV_RANK)), qabs8)
        qr_h = qrow[:, h * MLA_SLOT + HEAD_DIM:h * MLA_SLOT + HEAD_DIM + MLA_ROPE]
        qr8 = jnp.where(rowr == N_HEADS + h, jnp.broadcast_to(qr_h, (SUBLANES, MLA_ROPE)), qr8)
    qabs8 = qabs8.astype(BF16)
    qr8 = qr8.astype(BF16)
    cnew =cnew_ref[pl.ds(b, 1), :]
    eye8 = (lax.broadcasted_iota(jnp.int32, (SUBLANES, SUBLANES), 0)
            == lax.broadcasted_iota(jnp.int32, (SUBLANES, SUBLANES), 1))
    cnew_col = jnp.sum(jnp.where(eye8, jnp.broadcast_to(cnew, (SUBLANES, SUBLANES)), 0.0),
                       axis=-1, keepdims=True)

    def chunk(ci, carry):
        m, l, acc_f, acc_m = carry
        gidx = b * nc + ci
        slot = gidx % DECODE_RING
        nxt_b, nxt_c = chunk_of(gidx + depth)
        nxt_slot = (gidx + depth) % DECODE_RING

        wait_chunk(b, ci, slot)

        s_list = []
        zero_rows = jnp.zeros((SUBLANES - N_HEADS, page), F32)
        for j in range(g_pages):
            start_page(nxt_b, nxt_c, nxt_slot, j)
            s_f = jnp.sum(kbuf[slot, j] * qbc, axis=1)
            cp_ = cbuf[slot, j].astype(BF16)
            rt = rbuf[slot, j].astype(BF16)
            s = jnp.concatenate([s_f, zero_rows], axis=0) + _dot_nt(qabs8, cp_) + _dot(qr8, rt)
            s_list.append(s + bias_ref[ci * g_pages + j] + cnew_col)
        s = jnp.concatenate(s_list, axis=-1)
        m_new = jnp.maximum(m, jnp.max(s, axis=-1, keepdims=True))
        alpha = jnp.exp(m - m_new)
        p = jnp.exp(s - m_new)
        l = alpha * l + jnp.sum(p, axis=-1, keepdims=True)
        pb = p.astype(BF16)
        acc_f = acc_f * alpha[0:N_HEADS].reshape(N_HEADS, 1, 1)
        om = jnp.zeros((SUBLANES, MLA_KV_RANK), F32)
        for j in range(g_pages):
            pj = p[0:N_HEADS, j * page:(j + 1) * page]
            acc_f = acc_f + vbuf[slot, j] * pj[:, None, :]
            om = om + _dot(pb[:, j * page:(j + 1) * page], cbuf[slot, j].astype(BF16))
        return m_new, l, acc_f, alpha * acc_m + om

    init = (jnp.full((SUBLANES, 1), -jnp.inf, F32), jnp.zeros((SUBLANES, 1), F32),
            jnp.zeros((N_HEADS, HEAD_DIM, page), F32), jnp.zeros((SUBLANES, MLA_KV_RANK), F32))
    m, l, acc_f, acc_m = lax.fori_loop(0, nc, chunk, init)
    acc_row = lax.dot_general(jnp.ones((SUBLANES, page), F32), acc_f.reshape(gw, page),
                              (((1,), (1,)), ((), ())), precision=HIGHEST,
                              preferred_element_type=F32)[0:1, :]

    def head_row(col):
        return jnp.sum(jnp.where(head_mask, col, 0.0), axis=0, keepdims=True)

    knew = knew_ref[pl.ds(b, 1), :].astype(BF16).astype(F32)
    vnew = vnew_ref[pl.ds(b, 1), :].astype(BF16).astype(F32)
    ckvnew = ckvnew_ref[pl.ds(b, 1), :].astype(BF16).astype(F32)
    krnew = krnew_ref[pl.ds(b, 1), :].astype(BF16).astype(F32)
    s_new = (jnp.sum(qf8.astype(F32) * knew, axis=-1, keepdims=True)
             + jnp.sum(qabs8.astype(F32) * ckvnew, axis=-1, keepdims=True)
             + jnp.sum(qr8.astype(F32) * krnew, axis=-1, keepdims=True))
    m_fin = jnp.maximum(m, s_new)
    alpha = jnp.exp(m - m_fin)
    p_new = jnp.exp(s_new - m_fin)
    l = alpha * l + p_new
    p_new = p_new.astype(BF16).astype(F32)
    acc_m = alpha * acc_m + p_new * ckvnew
    inv = 1.0 / l
    fox_ref[pl.ds(b, 1), :] = (head_row(alpha) * acc_row + head_row(p_new) * vnew) * head_row(inv)
    olat = acc_m * inv
    for h in range(N_HEADS):
        olat_ref[h, pl.ds(b, 1), :] = olat[N_HEADS + h:N_HEADS + h + 1, :]

    @pl.when(b == nb - 1)
    def _():
        for g in range(depth):
            wait_chunk(nb_static - 1, nc - 1, (total + g) % DECODE_RING)
        for h in range(N_HEADS):
            mla_ref[:, h * HEAD_DIM:(h + 1) * HEAD_DIM] = _dot(olat_ref[h].astype(BF16), wuv_ref[h])


def _decode_call(page_table, bias, fq, qcat, cnew8, knew, vnew, ckvnew, krnew, wuk_h, wuv_h,
                 k_pool, v_pool, ckv_pool, kr_pool, layer):
    nb, n_pages = page_table.shape
    page = k_pool.shape[-1]
    g_pages = math.gcd(n_pages, DECODE_PAGES_PER_STEP)
    gw = GROUP_WIDTH
    vm = [fq, fq.T, qcat, cnew8, knew, vnew, ckvnew, krnew, wuk_h, wuv_h]
    in_specs = ([pl.BlockSpec((None, n_pages, SUBLANES, page), lambda b, pt: (b, 0, 0, 0))]
                + [pl.BlockSpec(a.shape, lambda b, pt, _n=a.ndim: (0,) * _n) for a in vm]
                + [pl.BlockSpec(memory_space=pl.ANY)] * 4)
    out_full = pl.BlockSpec((nb, gw), lambda b, pt: (0, 0))
    return pl.pallas_call(
        functools.partial(_decode_kernel, g_pages=g_pages, layer=layer),
        out_shape=[jax.ShapeDtypeStruct((nb, gw), F32), jax.ShapeDtypeStruct((nb, gw), F32)],
        grid_spec=pltpu.PrefetchScalarGridSpec(
            num_scalar_prefetch=1,
            grid=(nb,),
            in_specs=in_specs,
            out_specs=[out_full, out_full],
            scratch_shapes=[pltpu.VMEM((DECODE_RING, g_pages, N_HEADS, HEAD_DIM, page), F32),
                            pltpu.VMEM((DECODE_RING, g_pages, N_HEADS, HEAD_DIM, page), F32),
                            pltpu.VMEM((DECODE_RING, g_pages, page, MLA_KV_RANK), F32),
                            pltpu.VMEM((DECODE_RING, g_pages, MLA_ROPE, page), F32),
                            pltpu.SemaphoreType.DMA((4, DECODE_RING)),
                            pltpu.VMEM((N_HEADS, nb, MLA_KV_RANK), F32),
                            pltpu.VMEM((N_HEADS, nb, MLA_KV_RANK), F32)]),
        compiler_params=pltpu.CompilerParams(
            dimension_semantics=("arbitrary",), vmem_limit_bytes=VMEM_LIMIT_BYTES),
        name="decode_attn",
    )(page_table, bias, *vm, k_pool, v_pool, ckv_pool, kr_pool)


def _swap_halves(w):
    half = w.shape[-1] // 2
    return jnp.concatenate([w[..., half:], w[..., :half]], axis=-1)


def _layer_weights(p, l):
    gw = GROUP_WIDTH
    w_in = p['w_in'][l]
    sizes = (gw, gw, gw, N_HEADS, gw, gw, gw, gw, gw, GLA_LOWRANK, MLA_Q_RANK, MLA_KV_RANK, MLA_ROPE)
    offs = np.concatenate([[0], np.cumsum(sizes)])
    fq, fk, fv, ff, su, gq, gk, gv, gg, glr, mq, mkv, mkr = [
        w_in[:, int(offs[i]):int(offs[i + 1])] for i in range(len(sizes))]
    d = w_in.shape[0]
    small = jnp.concatenate([mkr, _swap_halves(mkr), glr, ff,
                             jnp.zeros((d, LANES - 2 * MLA_ROPE - GLA_LOWRANK - N_HEADS), F32)], axis=1)
    wcat = jnp.concatenate([fq, fk, fv, su, gq, gk, gv, gg, mkv, small, mq,
                            jnp.zeros((d, N_ZCOLS - C_MQ - MLA_Q_RANK), F32)], axis=1).astype(BF16)

    w_uq = p['w_mla_uq'][l].reshape(MLA_Q_RANK, N_HEADS, HEAD_DIM + MLA_ROPE)
    nope, ropew = w_uq[..., :HEAD_DIM], w_uq[..., HEAD_DIM:]
    zpad = jnp.zeros((MLA_Q_RANK, N_HEADS, MLA_SLOT - HEAD_DIM - MLA_ROPE), F32)
    w_uq_a = jnp.concatenate([nope, ropew, zpad], axis=-1).reshape(MLA_Q_RANK, N_HEADS * MLA_SLOT)
    w_uq_b = jnp.concatenate([jnp.zeros_like(nope), _swap_halves(ropew), zpad],
                             axis=-1).reshape(MLA_Q_RANK, N_HEADS * MLA_SLOT)
    w_uk = p['w_mla_uk'][l]
    w_uk_slots = jnp.concatenate(
        [w_uk, jnp.zeros((MLA_KV_RANK, N_HEADS, MLA_SLOT - HEAD_DIM), F32)],
        axis=-1).reshape(MLA_KV_RANK, N_HEADS * MLA_SLOT)
    p_kr = jnp.concatenate([jnp.zeros((MLA_ROPE, HEAD_DIM), F32), jnp.eye(MLA_ROPE, dtype=F32),
                            jnp.zeros((MLA_ROPE, MLA_SLOT - HEAD_DIM - MLA_ROPE), F32)], axis=1)
    p_kr = jnp.tile(p_kr, (1, N_HEADS))

    abr, abi, bin_re, bin_im = _s5_discretize(p['s5_a_re'][l], p['s5_a_im'][l], p['s5_log_dt'][l],
                                              p['s5_b_re'][l], p['s5_b_im'][l])
    row = lambda a: a.reshape(1, -1).astype(F32)
    return dict(
        wcat=wcat, norm_mix_pre=row(p['norm_mix_pre'][l]), b_fox_f=row(p['b_fox_f'][l]),
        w_gla_gk=p['w_gla_gk'][l].astype(BF16), b_gla_gk=row(p['b_gla_gk'][l]),
        mla_q_norm=row(p['mla_q_norm'][l]), w_uq_a=w_uq_a.astype(BF16), w_uq_b=w_uq_b.astype(BF16),
        mla_kv_norm=row(p['mla_kv_norm'][l]), w_uk_slots=w_uk_slots.astype(BF16),
        w_uv=p['w_mla_uv'][l].reshape(MLA_KV_RANK, gw).astype(BF16), p_kr=p_kr.astype(BF16),
        w_uk_h=jnp.transpose(w_uk, (1, 0, 2)).astype(BF16),
        w_uv_h=jnp.transpose(p['w_mla_uv'][l], (1, 0, 2)).astype(BF16),
        s5_abar_re=abr, s5_abar_im=abi, s5_bin_re=bin_re, s5_bin_im=bin_im,
        s5_cout_re=_s5_out_blockdiag(p['s5_c_re'][l]), s5_cout_im=_s5_out_blockdiag(p['s5_c_im'][l]),
        s5_d=row(p['s5_d'][l]), w_s5_glu=p['w_s5_glu'][l].astype(BF16), b_s5_glu=row(p['b_s5_glu'][l]),
        gla_norm=row(p['gla_norm'][l]),
        w_out=p['w_out'][l].astype(BF16), norm_mix_post=row(p['norm_mix_post'][l]),
        norm_ffn_pre=row(p['norm_ffn_pre'][l]), norm_ffn_post=row(p['norm_ffn_post'][l]),
        w_ffn_gate=p['w_ffn_gate'][l].astype(BF16), w_ffn_up=p['w_ffn_up'][l].astype(BF16),
        w_ffn_down=p['w_ffn_down'][l].astype(BF16))


def _rope_tables(pos):
    half = MLA_ROPE // 2
    inv = ROPE_THETA ** (-jnp.arange(half, dtype=F32) * 2.0 / MLA_ROPE)
    ang = pos.astype(F32)[:, None] * inv[None, :]
    cos, sin = jnp.cos(ang), jnp.sin(ang)
    n = pos.shape[0]
    c32 = jnp.concatenate([cos, cos], axis=-1)
    s32 = jnp.concatenate([-sin, sin], axis=-1)
    ones = jnp.ones((n, HEAD_DIM), F32)
    zeros = jnp.zeros((n, HEAD_DIM), F32)
    zpad = jnp.zeros((n, MLA_SLOT - HEAD_DIM - MLA_ROPE), F32)
    tqc = jnp.tile(jnp.concatenate([ones, c32, zpad], axis=-1), (1, N_HEADS))
    tqs = jnp.tile(jnp.concatenate([zeros, s32, zpad], axis=-1), (1, N_HEADS))
    return tqc, tqs, jnp.concatenate([c32, s32], axis=-1)


def kernel(x_prompt, x_sample, cache_fox_k, cache_fox_v, cache_fox_logf, cache_mla_ckv, cache_mla_krope, state_s5_re, state_s5_im, state_gla, page_table, meta_tokens, norm_mix_pre, norm_mix_post, norm_ffn_pre, norm_ffn_post, w_in, b_fox_f, s5_a_re, s5_a_im, s5_log_dt, s5_b_re, s5_b_im, s5_c_re, s5_c_im, s5_d, w_s5_glu, b_s5_glu, w_gla_gk, b_gla_gk, gla_norm, mla_q_norm, w_mla_uq, mla_kv_norm, w_mla_uk, w_mla_uv, w_out, w_ffn_gate, w_ffn_up, w_ffn_down):
    params = dict(
        norm_mix_pre=norm_mix_pre, norm_mix_post=norm_mix_post, norm_ffn_pre=norm_ffn_pre,
        norm_ffn_post=norm_ffn_post, w_in=w_in, b_fox_f=b_fox_f, s5_a_re=s5_a_re, s5_a_im=s5_a_im,
        s5_log_dt=s5_log_dt, s5_b_re=s5_b_re, s5_b_im=s5_b_im, s5_c_re=s5_c_re, s5_c_im=s5_c_im,
        s5_d=s5_d, w_s5_glu=w_s5_glu, b_s5_glu=b_s5_glu, w_gla_gk=w_gla_gk, b_gla_gk=b_gla_gk,
        gla_norm=gla_norm, mla_q_norm=mla_q_norm, w_mla_uq=w_mla_uq, mla_kv_norm=mla_kv_norm,
        w_mla_uk=w_mla_uk, w_mla_uv=w_mla_uv, w_out=w_out, w_ffn_gate=w_ffn_gate,
        w_ffn_up=w_ffn_up, w_ffn_down=w_ffn_down)
    depth = w_in.shape[0]
    nbp, seq, d = x_prompt.shape
    nbs, dec_seq, _ = x_sample.shape
    assert dec_seq == 1, "the sample path handles one new token per sequence"
    t = seq + N_META
    n_pages, page = page_table.shape[1], cache_fox_k.shape[2]
    past_len = n_pages * page
    gw = GROUP_WIDTH

    hp = jnp.concatenate([jnp.broadcast_to(meta_tokens.astype(x_prompt.dtype)[None], (nbp, N_META, d)),
                          x_prompt], axis=1)
    hs = x_sample.reshape(1, nbs, d)
    tabs_p = _rope_tables(jnp.arange(t))
    tabs_s = _rope_tables(jnp.full((nbs,), past_len, jnp.int32))
    k_pool_t = jnp.transpose(cache_fox_k, (0, 1, 3, 4, 2))
    v_pool_t = jnp.transpose(cache_fox_v, (0, 1, 3, 4, 2))
    kr_pool_t = jnp.transpose(cache_mla_krope, (0, 1, 3, 2))

    rows_p, rows_s = [], []
    for l in range(depth):
        lw = _layer_weights(params, l)

        (fq, fk, fv, fkb, fvb, logf, c, su_tm, gla_in, qcat, kcat, vmla, ckv, kr) = _proj_call(
            hp, lw, tabs_p, cumsum=True)
        fox_o = _attn_call(fq, fkb, fvb, c)
        mla_o = _attn_call(qcat, kcat, vmla)
        zeros_state = jnp.zeros((nbp, S5_LANES), F32)
        s5_o, s5_re, s5_im = _s5_call(su_tm.reshape(t * nbp, gw), zeros_state, zeros_state, lw, nbp)
        gla_o, gla_st = _gla_call(gla_in, jnp.zeros((nbp, gw, HEAD_DIM), F32), lw['gla_norm'])
        hp = _finish_call(hp, fox_o, s5_o.reshape(t, nbp * gw), gla_o, mla_o, lw)
        rows_p.append((fk.reshape(nbp, t, N_HEADS, HEAD_DIM), fv.reshape(nbp, t, N_HEADS, HEAD_DIM),
                       logf, ckv, kr, s5_re.reshape(nbp, S5_GROUPS, S5_STATE),
                       s5_im.reshape(nbp, S5_GROUPS, S5_STATE),
                       jnp.swapaxes(gla_st.reshape(nbp, N_HEADS, HEAD_DIM, HEAD_DIM), 2, 3)))

        (fq, fk, fv, fkb, fvb, logf, c, su_tm, gla_in, qcat, kcat, vmla, ckv, kr) = _proj_call(
            hs, lw, tabs_s, cumsum=False)
        logf_t = jnp.pad(jnp.swapaxes(cache_fox_logf[l].astype(F32), 1, 2),
                         ((0, 0), (0, SUBLANES - N_HEADS), (0, 0)))
        bias = _fox_bias_call(page_table, logf_t)
        cnew8 = jnp.concatenate([logf[0], jnp.zeros((nbs, SUBLANES - N_HEADS), F32)], axis=1)
        fox_o, mla_o = _decode_call(
            page_table, bias, fq[0].astype(F32), qcat[0].astype(F32), cnew8, fk[0], fv[0], ckv[0], kr[0],
            lw['w_uk_h'], lw['w_uv_h'], k_pool_t, v_pool_t, cache_mla_ckv, kr_pool_t, l)
        s5_o, s5_re, s5_im = _s5_call(su_tm, state_s5_re[l].reshape(nbs, S5_LANES),
                                      state_s5_im[l].reshape(nbs, S5_LANES), lw, nbs)
        gla_pad = jnp.concatenate([jnp.zeros((nbs, SUBLANES - 1, 5 * gw), F32),
                                   gla_in.reshape(nbs, 1, 5 * gw)], axis=1)
        s0t = jnp.swapaxes(state_gla[l].astype(F32), 2, 3).reshape(nbs, gw, HEAD_DIM)
        gla_o, gla_st = _gla_call(gla_pad, s0t, lw['gla_norm'])
        gla_o = gla_o[:, SUBLANES - 1, :].reshape(1, nbs, gw)
        hs = _finish_call(hs, fox_o.reshape(1, nbs, gw), s5_o, gla_o, mla_o.reshape(1, nbs, gw), lw)
        rows_s.append((fk.reshape(nbs, 1, N_HEADS, HEAD_DIM), fv.reshape(nbs, 1, N_HEADS, HEAD_DIM),
                       logf.reshape(nbs, 1, N_HEADS), ckv.reshape(nbs, 1, MLA_KV_RANK),
                       kr.reshape(nbs, 1, MLA_ROPE), s5_re.reshape(nbs, S5_GROUPS, S5_STATE),
                       s5_im.reshape(nbs, S5_GROUPS, S5_STATE),
                       jnp.swapaxes(gla_st.reshape(nbs, N_HEADS, HEAD_DIM, HEAD_DIM), 2, 3)))

    fk_p, fv_p, flf_p, ckv_p, kr_p, s5re_p, s5im_p, gla_p = [jnp.stack(r) for r in zip(*rows_p)]
    fk_s, fv_s, flf_s, ckv_s, kr_s, s5re_s, s5im_s, gla_s = [jnp.stack(r) for r in zip(*rows_s)]
    y_prompt = hp[:, N_META:]
    y_sample = hs.reshape(nbs, 1, d)
    return (y_prompt, y_sample, fk_p, fk_s, fv_p, fv_s, flf_p, flf_s, ckv_p, ckv_s, kr_p, kr_s,
            s5re_p, s5re_s, s5im_p, s5im_s, gla_p, gla_s)
```
